```python
import jax
import jax.numpy as jnp
from jax import lax
import numpy as np

D_MODEL = 1024
BATCH = 4
SEQ = 4096
DEPTH = 4
DEC_BATCH = 128
DEC_SEQ = 8
PAST_LEN = 2048
PAGE_SIZE = 128

N_MIXERS = 3
N_CM = (DEPTH + N_MIXERS - 1) // N_MIXERS
N_RWKV = (DEPTH + N_MIXERS - 2) // N_MIXERS
N_FOX = (DEPTH + N_MIXERS - 3) // N_MIXERS
NORM_EPS = 1e-6
CHUNK = 128
CM_WIDTH = 2 * D_MODEL
CM_GROUPS = 8
CM_GROUP_DIM = CM_WIDTH // CM_GROUPS
RW_HEAD = 64
RW_HEADS = D_MODEL // RW_HEAD
RW_DECAY_LORA = 64
RW_ICLR_LORA = 64
RW_GATE_LORA = 160
RW_GN_EPS = 64e-5
FOX_HEAD = 64
FOX_HEADS = D_MODEL // FOX_HEAD
Q_BLOCK = 128
N_EXPERTS = 32
TOP_K = 4
D_FF = D_MODEL
SWIGLU_LIMIT = 7.0
SWIGLU_ALPHA = 1.702
MOE_BLOCK = 128

kernel_name = 'hybrid_chunkmlp_rwkv7_fox_moe_adaln_step'


def rms_norm(x, g):
    xf = x.astype(jnp.float32)
    y = xf * lax.rsqrt(jnp.mean(xf * xf, axis=-1, keepdims=True) + NORM_EPS)
    return (y * g.astype(jnp.float32)).astype(x.dtype)


def standardize(x, eps):
    xf = x.astype(jnp.float32)
    mu = jnp.mean(xf, axis=-1, keepdims=True)
    var = jnp.mean(jnp.square(xf - mu), axis=-1, keepdims=True)
    return (xf - mu) * lax.rsqrt(var + eps)


def adaln_params(c, w, b):
    return jnp.split(jax.nn.silu(c) @ w + b, 6, axis=-1)


def modulate(h, shift, scale):
    return h * (1 + scale[:, None, :]) + shift[:, None, :]


def chunk_gating_mlp(h, chunk_len, w_in, b_in, ln_g, ln_b, w_s, b_s, w_out, b_out):
    bsz, t, _ = h.shape
    u, v = jnp.split(jax.nn.gelu(h @ w_in + b_in), 2, axis=-1)
    v = (standardize(v, NORM_EPS) * ln_g + ln_b).astype(h.dtype)
    causal = jnp.tril(jnp.ones((chunk_len, chunk_len), w_s.dtype))
    ws = w_s[:, :chunk_len, :chunk_len] * causal
    vc = v.reshape(bsz, t // chunk_len, chunk_len, CM_GROUPS, CM_GROUP_DIM)
    s = jnp.einsum('grq,bcqgd->bcrgd', ws, vc) + b_s[:, :chunk_len].T[None, None, :, :, None]
    y = (u * s.reshape(bsz, t, CM_WIDTH)) @ w_out + b_out
    return y, v


def wkv7_scan(state0, r, w, k, v, a, b):
    def step(state, inp):
        r_t, w_t, k_t, v_t, a_t, b_t = inp
        sa = jnp.einsum('bhij,bhj->bhi', state, a_t)
        state = state * w_t[:, :, None, :] + sa[..., None] * b_t[:, :, None, :] + v_t[..., None] * k_t[:, :, None, :]
        return state, jnp.einsum('bhij,bhj->bhi', state, r_t)
    seq = tuple(jnp.moveaxis(z, 1, 0) for z in (r, w, k, v, a, b))
    state, y = lax.scan(step, state0, seq)
    return state, jnp.moveaxis(y, 0, 1)


def rwkv7_time_mix(h, shift_prev, wkv_prev, mu, w_rkv, w_o, w0, w1, w2, a0, a1, a2, g1, g2, k_k, k_a, r_k, ln_g, ln_b):
    bsz, t, d = h.shape
    x_prev = jnp.concatenate([shift_prev[:, None, :].astype(h.dtype), h[:, :-1]], axis=1)
    xx = x_prev - h
    xr, xw, xk, xv, xa, xg = (h + xx * mu[n] for n in range(6))
    r = xr @ w_rkv[0]
    k = xk @ w_rkv[1]
    v = xv @ w_rkv[2]
    w_log = -jax.nn.softplus(-(w0 + jnp.tanh(xw @ w1) @ w2)) - 0.5
    decay = jnp.exp(-jnp.exp(w_log.astype(jnp.float32)))
    iclr = jax.nn.sigmoid(a0 + (xa @ a1) @ a2)
    gate = jax.nn.sigmoid(xg @ g1) @ g2

    def heads(z):
        return z.reshape(bsz, t, RW_HEADS, RW_HEAD).astype(jnp.float32)

    kk = heads(k * k_k)
    kk = kk / jnp.maximum(jnp.sqrt(jnp.sum(kk * kk, axis=-1, keepdims=True)), 1e-12)
    k = k * (1 + (iclr - 1) * k_a)
    rh, kh, vh, ah = heads(r), heads(k), heads(v), heads(iclr)
    state, y = wkv7_scan(wkv_prev.astype(jnp.float32), rh, heads(decay), kh, vh, -kk, kk * ah)
    y = standardize(y, RW_GN_EPS).reshape(bsz, t, d) * ln_g + ln_b
    y = y + (jnp.sum(rh * kh * r_k, axis=-1, keepdims=True) * vh).reshape(bsz, t, d)
    out = (y.astype(h.dtype) * gate) @ w_o
    return out, h[:, -1], state.astype(wkv_prev.dtype)


def fox_project(h, w_qkv, w_f, b_f):
    bsz, t, _ = h.shape
    q, k, v = jnp.split(h @ w_qkv, 3, axis=-1)
    shp = (bsz, t, FOX_HEADS, FOX_HEAD)
    logf = jax.nn.log_sigmoid((h @ w_f + b_f).astype(jnp.float32))
    return q.reshape(shp), k.reshape(shp), v.reshape(shp), logf


def fox_prompt(h, w_qkv, w_f, b_f, w_o):
    bsz, t, d = h.shape
    q, k, v, logf = fox_project(h, w_qkv, w_f, b_f)
    cum = jnp.cumsum(logf, axis=1).transpose(0, 2, 1)
    nb = t // Q_BLOCK
    q_blocks = q.reshape(bsz, nb, Q_BLOCK, FOX_HEADS, FOX_HEAD).transpose(1, 0, 2, 3, 4)
    cum_blocks = cum.reshape(bsz, FOX_HEADS, nb, Q_BLOCK).transpose(2, 0, 1, 3)
    key_pos = jnp.arange(t)
    scale = FOX_HEAD ** -0.5

    def block(args):
        blk, qb, cb = args
        s = jnp.einsum('bqhd,bkhd->bhqk', qb, k).astype(jnp.float32) * scale
        s = s + cb[..., :, None] - cum[:, :, None, :]
        q_pos = blk * Q_BLOCK + jnp.arange(Q_BLOCK)
        s = jnp.where(key_pos[None, :] <= q_pos[:, None], s, -jnp.inf)
        p = jax.nn.softmax(s, axis=-1).astype(v.dtype)
        return jnp.einsum('bhqk,bkhd->bqhd', p, v)

    o = lax.map(block, (jnp.arange(nb), q_blocks, cum_blocks))
    o = o.transpose(1, 0, 2, 3, 4).reshape(bsz, t, d)
    return o @ w_o, k, v, logf.astype(h.dtype)


def fox_sample(h, cache_k, cache_v, cache_logf, page_table, w_qkv, w_f, b_f, w_o):
    bsz, t, d = h.shape
    n_pages = page_table.shape[1]
    q, k, v, logf = fox_project(h, w_qkv, w_f, b_f)
    qf = q.astype(jnp.float32) * FOX_HEAD ** -0.5
    g_new = jnp.cumsum(logf, axis=1).transpose(0, 2, 1)
    past_lf = cache_logf[page_table].astype(jnp.float32).reshape(bsz, n_pages * PAGE_SIZE, FOX_HEADS)
    past_cum = jnp.cumsum(past_lf, axis=1)
    tail = (past_cum[:, -1:, :] - past_cum).reshape(bsz, n_pages, PAGE_SIZE, FOX_HEADS).transpose(1, 0, 3, 2)
    s = jnp.einsum('bthd,buhd->bhtu', qf, k.astype(jnp.float32)) + g_new[..., :, None] - g_new[..., None, :]
    s = jnp.where(jnp.tril(jnp.ones((t, t), bool)), s, -jnp.inf)
    m = jnp.max(s, axis=-1)
    p = jnp.exp(s - m[..., None])
    l = jnp.sum(p, axis=-1)
    acc = jnp.einsum('bhtu,buhd->bhtd', p, v.astype(jnp.float32))

    def page_step(carry, xs):
        m, l, acc = carry
        phys, tail_p = xs
        kp = cache_k[phys].astype(jnp.float32)
        vp = cache_v[phys].astype(jnp.float32)
        s = jnp.einsum('bthd,bphd->bhtp', qf, kp) + tail_p[:, :, None, :] + g_new[..., None]
        m_new = jnp.maximum(m, jnp.max(s, axis=-1))
        corr = jnp.exp(m - m_new)
        p = jnp.exp(s - m_new[..., None])
        acc = acc * corr[..., None] + jnp.einsum('bhtp,bphd->bhtd', p, vp)
        return (m_new, l * corr + jnp.sum(p, axis=-1), acc), None

    (m, l, acc), _ = lax.scan(page_step, (m, l, acc), (page_table.T, tail))
    o = (acc / l[..., None]).transpose(0, 2, 1, 3).reshape(bsz, t, d).astype(h.dtype)
    return o @ w_o, k, v, logf.astype(h.dtype)


def moe_ffn(x, w_r, b_r, w1, b1, w2, b2):
    n, d = x.shape
    logits = (x @ w_r + b_r).astype(jnp.float32)
    top_val, top_idx = lax.top_k(logits, TOP_K)
    gates = jax.nn.softmax(top_val, axis=-1)
    n_rows = n * TOP_K
    e_flat = top_idx.reshape(-1)
    order = jnp.argsort(e_flat)
    e_sorted = e_flat[order]
    tok_sorted = (order // TOP_K).astype(jnp.int32)
    g_sorted = gates.reshape(-1)[order]
    counts = jnp.bincount(e_flat, length=N_EXPERTS)
    padded = (counts + MOE_BLOCK - 1) // MOE_BLOCK * MOE_BLOCK
    starts = jnp.cumsum(counts) - counts
    pad_ends = jnp.cumsum(padded)
    pad_starts = pad_ends - padded
    dest = pad_starts[e_sorted] + jnp.arange(n_rows) - starts[e_sorted]
    n_blocks = -(-n_rows // MOE_BLOCK) + N_EXPERTS
    n_slots = n_blocks * MOE_BLOCK
    slot_tok = jnp.full((n_slots,), n, jnp.int32).at[dest].set(tok_sorted)
    slot_gate = jnp.zeros((n_slots,), jnp.float32).at[dest].set(g_sorted)
    block_expert = jnp.minimum(jnp.searchsorted(pad_ends, jnp.arange(n_blocks) * MOE_BLOCK, side='right'), N_EXPERTS - 1)
    x_pad = jnp.concatenate([x, jnp.zeros((1, d), x.dtype)], axis=0)
    x_blocks = x_pad[slot_tok].reshape(n_blocks, MOE_BLOCK, d)

    def expert_block(args):
        xb, e = args
        gate, up = jnp.split(xb @ w1[e] + b1[e], 2, axis=-1)
        gate = jnp.minimum(gate, SWIGLU_LIMIT)
        up = jnp.clip(up, -SWIGLU_LIMIT, SWIGLU_LIMIT)
        return ((up + 1) * (gate * jax.nn.sigmoid(SWIGLU_ALPHA * gate))) @ w2[e] + b2[e]

    y_slots = lax.map(expert_block, (x_blocks, block_expert)).reshape(n_slots, d)
    y = jax.ops.segment_sum(y_slots * slot_gate[:, None].astype(y_slots.dtype), slot_tok, num_segments=n + 1)
    return y[:n]


def setup_inputs(seed: int = 0) -> dict:
    key = jax.random.key(seed)
    ks = iter(jax.random.split(key, 64))

    def nrm(shape, s=1.0):
        return s * jax.random.normal(next(ks), shape, jnp.float32)

    def uni(shape, lo, hi):
        return jax.random.uniform(next(ks), shape, jnp.float32, lo, hi)

    D = D_MODEL
    n_pages = PAST_LEN // PAGE_SIZE
    n_used = DEC_BATCH * n_pages
    n_pool = n_used + max(1, n_used // 4)
    page_table = jax.random.permutation(next(ks), n_pool)[:n_used].reshape(DEC_BATCH, n_pages).astype(jnp.int32)
    return {
        'x_prompt': nrm((BATCH, SEQ, D)),
        'x_sample': nrm((DEC_BATCH, DEC_SEQ, D)),
        'cache_fox_k': nrm((N_FOX, n_pool, PAGE_SIZE, FOX_HEADS, FOX_HEAD)),
        'cache_fox_v': nrm((N_FOX, n_pool, PAGE_SIZE, FOX_HEADS, FOX_HEAD)),
        'cache_fox_logf': jax.nn.log_sigmoid(2.5 + nrm((N_FOX, n_pool, PAGE_SIZE, FOX_HEADS))),
        'state_rwkv_shift': nrm((N_RWKV, DEC_BATCH, D)),
        'state_rwkv_wkv': nrm((N_RWKV, DEC_BATCH, RW_HEADS, RW_HEAD, RW_HEAD), 0.1),
        'page_table': page_table,
        'c_prompt': nrm((BATCH, D)),
        'c_sample': nrm((DEC_BATCH, D)),
        'norm1_g': 1.0 + nrm((DEPTH, D), 0.05),
        'norm2_g': 1.0 + nrm((DEPTH, D), 0.05),
        'ada_w': nrm((DEPTH, D, 6 * D), 0.5 * D ** -0.5),
        'ada_b': nrm((DEPTH, 6 * D), 0.02),
        'cm_w_in': nrm((N_CM, D, 2 * CM_WIDTH), D ** -0.5),
        'cm_b_in': nrm((N_CM, 2 * CM_WIDTH), 0.02),
        'cm_ln_g': 1.0 + nrm((N_CM, CM_WIDTH), 0.05),
        'cm_ln_b': nrm((N_CM, CM_WIDTH), 0.02),
        'cm_w_s': nrm((N_CM, CM_GROUPS, CHUNK, CHUNK), CHUNK ** -0.5),
        'cm_b_s': 1.0 + nrm((N_CM, CM_GROUPS, CHUNK), 0.1),
        'cm_w_out': nrm((N_CM, CM_WIDTH, D), CM_WIDTH ** -0.5),
        'cm_b_out': nrm((N_CM, D), 0.02),
        'rw_mu': uni((N_RWKV, 6, D), 0.0, 1.0),
        'rw_w_rkv': nrm((N_RWKV, 3, D, D), D ** -0.5),
        'rw_w_o': nrm((N_RWKV, D, D), D ** -0.5),
        'rw_w0': uni((N_RWKV, D), -6.5, -1.0),
        'rw_w1': nrm((N_RWKV, D, RW_DECAY_LORA), 0.5 * D ** -0.5),
        'rw_w2': nrm((N_RWKV, RW_DECAY_LORA, D), 0.5 * RW_DECAY_LORA ** -0.5),
        'rw_a0': nrm((N_RWKV, D), 0.1),
        'rw_a1': nrm((N_RWKV, D, RW_ICLR_LORA), 0.5 * D ** -0.5),
        'rw_a2': nrm((N_RWKV, RW_ICLR_LORA, D), 0.5 * RW_ICLR_LORA ** -0.5),
        'rw_g1': nrm((N_RWKV, D, RW_GATE_LORA), D ** -0.5),
        'rw_g2': nrm((N_RWKV, RW_GATE_LORA, D), RW_GATE_LORA ** -0.5),
        'rw_k_k': 0.85 + nrm((N_RWKV, D), 0.05),
        'rw_k_a': 1.0 + nrm((N_RWKV, D), 0.05),
        'rw_r_k': -0.04 + nrm((N_RWKV, RW_HEADS, RW_HEAD), 0.02),
        'rw_ln_g': 1.0 + nrm((N_RWKV, D), 0.05),
        'rw_ln_b': nrm((N_RWKV, D), 0.02),
        'fox_w_qkv': nrm((N_FOX, D, 3 * D), D ** -0.5),
        'fox_w_f': nrm((N_FOX, D, FOX_HEADS), 0.5 * D ** -0.5),
        'fox_b_f': uni((N_FOX, FOX_HEADS), 1.0, 4.0),
        'fox_w_o': nrm((N_FOX, D, D), D ** -0.5),
        'moe_wr': nrm((DEPTH, D, N_EXPERTS), D ** -0.5),
        'moe_br': nrm((DEPTH, N_EXPERTS), 0.01),
        'moe_w1': nrm((DEPTH, N_EXPERTS, D, 2 * D_FF), D ** -0.5),
        'moe_b1': nrm((DEPTH, N_EXPERTS, 2 * D_FF), 0.02),
        'moe_w2': nrm((DEPTH, N_EXPERTS, D_FF, D), D_FF ** -0.5),
        'moe_b2': nrm((DEPTH, N_EXPERTS, D), 0.02),
        'final_g': 1.0 + nrm((D,), 0.05),
    }


def reference(x_prompt, x_sample, cache_fox_k, cache_fox_v, cache_fox_logf, state_rwkv_shift, state_rwkv_wkv,
              page_table, c_prompt, c_sample, norm1_g, norm2_g, ada_w, ada_b,
              cm_w_in, cm_b_in, cm_ln_g, cm_ln_b, cm_w_s, cm_b_s, cm_w_out, cm_b_out,
              rw_mu, rw_w_rkv, rw_w_o, rw_w0, rw_w1, rw_w2, rw_a0, rw_a1, rw_a2, rw_g1, rw_g2,
              rw_k_k, rw_k_a, rw_r_k, rw_ln_g, rw_ln_b,
              fox_w_qkv, fox_w_f, fox_b_f, fox_w_o,
              moe_wr, moe_br, moe_w1, moe_b1, moe_w2, moe_b2, final_g):
    xp, xs = x_prompt, x_sample
    n_prompt = xp.shape[0] * xp.shape[1]
    cm_v_s = []
    rw_sh_p, rw_wkv_p, rw_sh_s, rw_wkv_s = [], [], [], []
    fx_k_p, fx_v_p, fx_lf_p, fx_k_s, fx_v_s, fx_lf_s = [], [], [], [], [], []
    for i in range(DEPTH):
        kind, j = i % N_MIXERS, i // N_MIXERS
        mod_p = adaln_params(c_prompt, ada_w[i], ada_b[i])
        mod_s = adaln_params(c_sample, ada_w[i], ada_b[i])
        hp = modulate(rms_norm(xp, norm1_g[i]), mod_p[0], mod_p[1])
        hs = modulate(rms_norm(xs, norm1_g[i]), mod_s[0], mod_s[1])
        if kind == 0:
            cm = (cm_w_in[j], cm_b_in[j], cm_ln_g[j], cm_ln_b[j], cm_w_s[j], cm_b_s[j], cm_w_out[j], cm_b_out[j])
            op, _ = chunk_gating_mlp(hp, CHUNK, *cm)
            os_, v_new = chunk_gating_mlp(hs, hs.shape[1], *cm)
            cm_v_s.append(v_new)
        elif kind == 1:
            rw = (rw_mu[j], rw_w_rkv[j], rw_w_o[j], rw_w0[j], rw_w1[j], rw_w2[j], rw_a0[j], rw_a1[j], rw_a2[j],
                  rw_g1[j], rw_g2[j], rw_k_k[j], rw_k_a[j], rw_r_k[j], rw_ln_g[j], rw_ln_b[j])
            bp = xp.shape[0]
            op, sh_p, wkv_p = rwkv7_time_mix(hp, jnp.zeros((bp, D_MODEL), hp.dtype),
                                             jnp.zeros((bp, RW_HEADS, RW_HEAD, RW_HEAD), state_rwkv_wkv.dtype), *rw)
            os_, sh_s, wkv_s = rwkv7_time_mix(hs, state_rwkv_shift[j], state_rwkv_wkv[j], *rw)
            rw_sh_p.append(sh_p)
            rw_wkv_p.append(wkv_p)
            rw_sh_s.append(sh_s)
            rw_wkv_s.append(wkv_s)
        else:
            op, k_p, v_p, lf_p = fox_prompt(hp, fox_w_qkv[j], fox_w_f[j], fox_b_f[j], fox_w_o[j])
            os_, k_s, v_s, lf_s = fox_sample(hs, cache_fox_k[j], cache_fox_v[j], cache_fox_logf[j], page_table,
                                             fox_w_qkv[j], fox_w_f[j], fox_b_f[j], fox_w_o[j])
            fx_k_p.append(k_p)
            fx_v_p.append(v_p)
            fx_lf_p.append(lf_p)
            fx_k_s.append(k_s)
            fx_v_s.append(v_s)
            fx_lf_s.append(lf_s)
        xp = xp + mod_p[2][:, None, :] * op
        xs = xs + mod_s[2][:, None, :] * os_
        hp = modulate(rms_norm(xp, norm2_g[i]), mod_p[3], mod_p[4])
        hs = modulate(rms_norm(xs, norm2_g[i]), mod_s[3], mod_s[4])
        h_all = jnp.concatenate([hp.reshape(-1, D_MODEL), hs.reshape(-1, D_MODEL)], axis=0)
        f_all = moe_ffn(h_all, moe_wr[i], moe_br[i], moe_w1[i], moe_b1[i], moe_w2[i], moe_b2[i])
        xp = xp + mod_p[5][:, None, :] * f_all[:n_prompt].reshape(xp.shape)
        xs = xs + mod_s[5][:, None, :] * f_all[n_prompt:].reshape(xs.shape)
    y_prompt = rms_norm(xp, final_g)
    y_sample = rms_norm(xs, final_g)
    return (y_prompt, y_sample, jnp.stack(cm_v_s),
            jnp.stack(rw_sh_p), jnp.stack(rw_wkv_p), jnp.stack(rw_sh_s), jnp.stack(rw_wkv_s),
            jnp.stack(fx_k_p), jnp.stack(fx_v_p), jnp.stack(fx_lf_p),
            jnp.stack(fx_k_s), jnp.stack(fx_v_s), jnp.stack(fx_lf_s))
```

```python
import functools

import jax
import jax.numpy as jnp
from jax import lax
from jax.experimental import pallas as pl
from jax.experimental.pallas import tpu as pltpu

F32 = jnp.float32
BF16 = jnp.bfloat16
HIGHEST = lax.Precision.HIGHEST

NORM_EPS = 1e-6
N_MIXERS = 3
CHUNK = 128
CM_GROUPS = 8
RW_HEAD = 64
RW_GN_EPS = 64e-5
FOX_HEAD = 64
PAGE_SIZE = 128
TOP_K = 4
SWIGLU_LIMIT = 7.0
SWIGLU_ALPHA = 1.702
MOE_BLOCK = 128

V7X_VMEM_BYTES = 64 * 1024 * 1024
VMEM_LIMIT = V7X_VMEM_BYTES - 8 * 1024 * 1024
LANES = 128

ROW_TILE = 256
FLASH_TILE = 256
WKV_CHUNK = 64
WKV_TILE = 512
MOE_ROWS = 256


def _params(*sem):
    return pltpu.CompilerParams(dimension_semantics=sem, vmem_limit_bytes=VMEM_LIMIT)


def _bdot(a, b):
    return jnp.dot(a.astype(BF16), b.astype(BF16), preferred_element_type=F32)


def _norm_mod(x, g, shift, scale):
    y = x * lax.rsqrt(jnp.mean(x * x, axis=-1, keepdims=True) + NORM_EPS)
    return (y * g) * (1.0 + scale) + shift


def _tile(n, want):
    t = min(n, want)
    assert n % t == 0, (n, want)
    return t


class _Rows:
    def __init__(self, n_seq, seq_len, d, want_tile):
        self.n_seq, self.seq_len, self.d = n_seq, seq_len, d
        self.n = n_seq * seq_len
        self.per_seq = seq_len >= want_tile or seq_len >= 128
        if self.per_seq:
            self.tm = _tile(seq_len, want_tile)
        else:
            self.tm = _tile(self.n, want_tile)
            assert self.tm % seq_len == 0
        self.grid = self.n // self.tm

    def x_spec(self, width=None):
        return pl.BlockSpec((self.tm, width or self.d), lambda i: (i, 0))

    def mod_spec(self):
        if self.per_seq:
            per = self.seq_len // self.tm
            return pl.BlockSpec((None, 1, self.d), lambda i: (i // per, 0, 0))
        return pl.BlockSpec((self.tm, self.d), lambda i: (i, 0))

    def mod(self, m):
        if self.per_seq:
            return m.reshape(self.n_seq, 1, self.d)
        return jnp.repeat(m, self.seq_len, axis=0)


def _const_spec(shape):
    nd = len(shape)
    return pl.BlockSpec(shape, lambda *_: (0,) * nd)


def _adaln_kernel(c_ref, w_ref, b_ref, o_ref):
    c = c_ref[...]
    s = c * jax.nn.sigmoid(c)
    o_ref[0] = _bdot(s, w_ref[0]) + b_ref[0]


def _adaln(c_all, ada_w, ada_b):
    depth, d, d6 = ada_w.shape
    n = c_all.shape[0]
    tn = _tile(d6, 1536)
    return pl.pallas_call(
        _adaln_kernel,
        grid=(depth, d6 // tn),
        in_specs=[
            pl.BlockSpec((n, d), lambda i, j: (0, 0)),
            pl.BlockSpec((1, d, tn), lambda i, j: (i, 0, j)),
            pl.BlockSpec((1, 1, tn), lambda i, j: (i, 0, j)),
        ],
        out_specs=pl.BlockSpec((1, n, tn), lambda i, j: (i, 0, j)),
        out_shape=jax.ShapeDtypeStruct((depth, n, d6), F32),
        compiler_params=_params("arbitrary", "arbitrary"),
        name="adaln",
    )(c_all, ada_w, ada_b.reshape(depth, 1, d6))


def _cm_kernel(emit_v, x_ref, g_ref, sh_ref, sc_ref, gt_ref, win_ref, bin_ref, lng_ref, lnb_ref, ws_ref, bs_ref,
               wout_ref, bout_ref, xo_ref, *rest):
    if emit_v:
        v_ref, gated_ref = rest
    else:
        (gated_ref,) = rest
    tm = x_ref.shape[0]
    width = lng_ref.shape[-1]
    gd = width // CM_GROUPS
    x = x_ref[...]
    h = _norm_mod(x, g_ref[...], sh_ref[...], sc_ref[...])
    z = jax.nn.gelu(_bdot(h, win_ref[...]) + bin_ref[...])
    u = z[:, :width]
    v = z[:, width:]
    mu = jnp.mean(v, axis=-1, keepdims=True)
    vc = v - mu
    var = jnp.mean(vc * vc, axis=-1, keepdims=True)
    vn = vc * lax.rsqrt(var + NORM_EPS) * lng_ref[...] + lnb_ref[...]
    if emit_v:
        v_ref[...] = vn
    vb = vn.astype(BF16)
    row = lax.broadcasted_iota(jnp.int32, (CHUNK, CHUNK), 0)
    col = lax.broadcasted_iota(jnp.int32, (CHUNK, CHUNK), 1)
    causal = col <= row
    for g in range(CM_GROUPS):
        wg = jnp.where(causal, ws_ref[g], 0.0).astype(BF16)
        for c in range(tm // CHUNK):
            r0 = c * CHUNK
            s = jnp.dot(wg, vb[r0:r0 + CHUNK, g * gd:(g + 1) * gd], preferred_element_type=F32)
            s = s + bs_ref[:, g * gd:(g + 1) * gd]
            gated_ref[r0:r0 + CHUNK, g * gd:(g + 1) * gd] = (u[r0:r0 + CHUNK, g * gd:(g + 1) * gd] * s).astype(BF16)
    y = jnp.dot(gated_ref[...], wout_ref[...], preferred_element_type=F32) + bout_ref[...]
    xo_ref[...] = x + gt_ref[...] * y


def _cm_layer(rows, x, g, shift, scale, gate, w_in, b_in, ln_g, ln_b, ws_bd, bs_rows, w_out, b_out, emit_v):
    d = rows.d
    width = w_out.shape[0]
    tm = rows.tm
    assert tm % CHUNK == 0
    out_shape = [jax.ShapeDtypeStruct((rows.n, d), F32)]
    out_specs = [rows.x_spec()]
    if emit_v:
        out_shape.append(jax.ShapeDtypeStruct((rows.n, width), F32))
        out_specs.append(rows.x_spec(width))
    res = pl.pallas_call(
        functools.partial(_cm_kernel, emit_v),
        grid=(rows.grid,),
        in_specs=[
            rows.x_spec(), _const_spec((1, d)), rows.mod_spec(), rows.mod_spec(), rows.mod_spec(),
            _const_spec((d, 2 * width)), _const_spec((1, 2 * width)), _const_spec((1, width)), _const_spec((1, width)),
            _const_spec((CM_GROUPS, CHUNK, CHUNK)), _const_spec((CHUNK, width)),
            _const_spec((width, d)), _const_spec((1, d)),
        ],
        out_specs=out_specs,
        out_shape=out_shape,
        scratch_shapes=[pltpu.VMEM((tm, width), BF16)],
        compiler_params=_params("arbitrary"),
        name="cm_layer_v" if emit_v else "cm_layer",
    )(x, g.reshape(1, d), rows.mod(shift), rows.mod(scale), rows.mod(gate),
      w_in.astype(BF16), b_in.reshape(1, -1), ln_g.reshape(1, -1), ln_b.reshape(1, -1),
      ws_bd, bs_rows, w_out.astype(BF16), b_out.reshape(1, d))
    return res


def _cm_spatial_weights(w_s, b_s, chunk_len, width):
    groups = w_s.shape[0]
    rep = CHUNK // chunk_len
    ws = w_s[:, :chunk_len, :chunk_len]
    if rep > 1:
        eye = jnp.eye(rep, dtype=w_s.dtype)
        ws = jnp.einsum("ab,grq->garbq", eye, ws).reshape(groups, CHUNK, CHUNK)
    bs = jnp.tile(b_s[:, :chunk_len].T, (rep, 1))
    return ws, jnp.repeat(bs, width // groups, axis=1)


NEG_BIG = -1e30


def _router_kernel(x_ref, g_ref, sh_ref, sc_ref, wr_ref, br_ref, cin_ref,
                   h_ref, idx_ref, gate_ref, rank_ref, cnt_ref):
    i = pl.program_id(0)
    tm = x_ref.shape[0]

    @pl.when(i == 0)
    def _():
        cnt_ref[...] = cin_ref[...]

    h = _norm_mod(x_ref[...], g_ref[...], sh_ref[...], sc_ref[...])
    h_ref[...] = h.astype(BF16)
    logits = jnp.dot(h, wr_ref[...], precision=HIGHEST, preferred_element_type=F32) + br_ref[...]
    col = lax.broadcasted_iota(jnp.int32, (tm, LANES), 1).astype(F32)
    work = logits
    hits = jnp.zeros((tm, LANES), F32)
    vals, sels = [], []
    for _ in range(TOP_K):
        m = jnp.max(work, axis=-1, keepdims=True)
        sel = jnp.min(jnp.where(work == m, col, float(LANES)), axis=-1, keepdims=True)
        hit = col == sel
        hits = hits + hit.astype(F32)
        work = jnp.where(hit, -jnp.inf, work)
        vals.append(m)
        sels.append(sel)
    es = [jnp.exp(v - vals[0]) for v in vals]
    tot = es[0] + es[1] + es[2] + es[3]
    r = lax.broadcasted_iota(jnp.int32, (tm, tm), 0)
    c = lax.broadcasted_iota(jnp.int32, (tm, tm), 1)
    below = jnp.where(c < r, 1.0, 0.0).astype(BF16)
    prefix = jnp.dot(below, hits.astype(BF16), preferred_element_type=F32) + cnt_ref[...]
    idx_o = jnp.zeros((tm, LANES), F32)
    gate_o = jnp.zeros((tm, LANES), F32)
    rank_o = jnp.zeros((tm, LANES), F32)
    for k in range(TOP_K):
        rank_k = jnp.sum(jnp.where(col == sels[k], prefix, 0.0), axis=-1, keepdims=True)
        idx_o = jnp.where(col == float(k), sels[k], idx_o)
        gate_o = jnp.where(col == float(k), es[k] / tot, gate_o)
        rank_o = jnp.where(col == float(k), rank_k, rank_o)
    idx_ref[...] = idx_o.astype(jnp.int32)
    gate_ref[...] = gate_o
    rank_ref[...] = rank_o.astype(jnp.int32)
    cnt_ref[...] = cnt_ref[...] + jnp.sum(hits, axis=0, keepdims=True)


def _router(rows, x, g, shift, scale, w_r, b_r, counts_in):
    d = rows.d
    n_exp = w_r.shape[1]
    wr = jnp.pad(w_r, ((0, 0), (0, LANES - n_exp)))
    br = jnp.pad(b_r, (0, LANES - n_exp), constant_values=NEG_BIG).reshape(1, LANES)
    lane_spec = pl.BlockSpec((rows.tm, LANES), lambda i: (i, 0))
    return pl.pallas_call(
        _router_kernel,
        grid=(rows.grid,),
        in_specs=[rows.x_spec(), _const_spec((1, d)), rows.mod_spec(), rows.mod_spec(),
                  _const_spec((d, LANES)), _const_spec((1, LANES)), _const_spec((1, LANES))],
        out_specs=[rows.x_spec(), lane_spec, lane_spec, lane_spec, _const_spec((1, LANES))],
        out_shape=[jax.ShapeDtypeStruct((rows.n, d), BF16),
                   jax.ShapeDtypeStruct((rows.n, LANES), jnp.int32),
                   jax.ShapeDtypeStruct((rows.n, LANES), F32),
                   jax.ShapeDtypeStruct((rows.n, LANES), jnp.int32),
                   jax.ShapeDtypeStruct((1, LANES), F32)],
        compiler_params=_params("arbitrary"),
        name="moe_router",
    )(x, g.reshape(1, d), rows.mod(shift), rows.mod(scale), wr, br, counts_in)


def _experts_kernel(be_ref, nu_ref, x_ref, w1_ref, b1_ref, w2_ref, b2_ref, y_ref, w1b_ref, w2b_ref):
    i = pl.program_id(0)
    f = w2_ref.shape[2]
    new_expert = jnp.logical_or(i == 0, be_ref[i] != be_ref[jnp.maximum(i - 1, 0)])

    @pl.when(jnp.logical_and(new_expert, i < nu_ref[0]))
    def _():
        w1b_ref[...] = w1_ref[0, 0].astype(BF16)
        w2b_ref[...] = w2_ref[0, 0].astype(BF16)

    @pl.when(i < nu_ref[0])
    def _():
        hcat = jnp.dot(x_ref[...], w1b_ref[...], preferred_element_type=F32) + b1_ref[0, 0]
        gate = jnp.minimum(hcat[:, :f], SWIGLU_LIMIT)
        up = jnp.clip(hcat[:, f:], -SWIGLU_LIMIT, SWIGLU_LIMIT)
        act = (up + 1.0) * (gate * jax.nn.sigmoid(SWIGLU_ALPHA * gate))
        y_ref[...] = jnp.dot(act.astype(BF16), w2b_ref[...], preferred_element_type=F32) + b2_ref[0, 0]


def _experts(x_sorted, block_expert, n_used, w1, b1, w2, b2, layer, tmb):
    n_slots, d = x_sorted.shape
    _, n_exp, _, f2 = w1.shape
    f = f2 // 2
    n_blocks = n_slots // tmb
    grid_spec = pltpu.PrefetchScalarGridSpec(
        num_scalar_prefetch=2,
        grid=(n_blocks,),
        in_specs=[
            pl.BlockSpec((tmb, d), lambda i, be, nu: (i, 0)),
            pl.BlockSpec((1, 1, d, f2), lambda i, be, nu: (layer, be[i], 0, 0)),
            pl.BlockSpec((1, 1, 1, f2), lambda i, be, nu: (layer, be[i], 0, 0)),
            pl.BlockSpec((1, 1, f, d), lambda i, be, nu: (layer, be[i], 0, 0)),
            pl.BlockSpec((1, 1, 1, d), lambda i, be, nu: (layer, be[i], 0, 0)),
        ],
        out_specs=pl.BlockSpec((tmb, d), lambda i, be, nu: (i, 0)),
        scratch_shapes=[pltpu.VMEM((d, f2), BF16), pltpu.VMEM((f, d), BF16)],
    )
    return pl.pallas_call(
        _experts_kernel,
        grid_spec=grid_spec,
        out_shape=jax.ShapeDtypeStruct((n_slots, d), F32),
        compiler_params=_params("arbitrary"),
        name="moe_experts",
    )(block_expert, n_used, x_sorted, w1, b1.reshape(b1.shape[0], n_exp, 1, f2), w2, b2.reshape(b2.shape[0], n_exp, 1, d))


def _combine_kernel(final, x_ref, gt_ref, yg_ref, gates_ref, *rest):
    if final:
        fg_ref, o_ref = rest
    else:
        (o_ref,) = rest
    d = x_ref.shape[1]
    gates = gates_ref[...]
    y = gates[:, 0:1] * yg_ref[:, 0:d]
    for k in range(1, TOP_K):
        y = y + gates[:, k:k + 1] * yg_ref[:, k * d:(k + 1) * d]
    xn = x_ref[...] + gt_ref[...] * y
    if final:
        xn = xn * lax.rsqrt(jnp.mean(xn * xn, axis=-1, keepdims=True) + NORM_EPS) * fg_ref[...]
    o_ref[...] = xn


def _combine(rows, x, gate2, y_gathered, gates, final_g):
    d = rows.d
    final = final_g is not None
    in_specs = [rows.x_spec(), rows.mod_spec(), rows.x_spec(TOP_K * d), pl.BlockSpec((rows.tm, LANES), lambda i: (i, 0))]
    args = [x, rows.mod(gate2), y_gathered, gates]
    if final:
        in_specs.append(_const_spec((1, d)))
        args.append(final_g.reshape(1, d))
    return pl.pallas_call(
        functools.partial(_combine_kernel, final),
        grid=(rows.grid,),
        in_specs=in_specs,
        out_specs=rows.x_spec(),
        out_shape=jax.ShapeDtypeStruct((rows.n, d), F32),
        compiler_params=_params("arbitrary"),
        name="moe_combine",
    )(*args)


def _moe_layer(layer, rows_p, rows_s, xp, xs, g, mod_p, mod_s, w_r, b_r, w1, b1, w2, b2, final_g, tmb):
    d = rows_p.d
    n_exp = w_r.shape[1]
    zero = jnp.zeros((1, LANES), F32)
    hp, idx_p, gates_p, rank_p, cnt = _router(rows_p, xp, g, mod_p[3], mod_p[4], w_r, b_r, zero)
    hs, idx_s, gates_s, rank_s, cnt = _router(rows_s, xs, g, mod_s[3], mod_s[4], w_r, b_r, cnt)
    n_p, n = rows_p.n, rows_p.n + rows_s.n
    idx = jnp.concatenate([idx_p[:, :TOP_K], idx_s[:, :TOP_K]], 0)
    rank = jnp.concatenate([rank_p[:, :TOP_K], rank_s[:, :TOP_K]], 0)
    counts = cnt[0, :n_exp].astype(jnp.int32)
    padded = (counts + tmb - 1) // tmb * tmb
    pad_ends = jnp.cumsum(padded)
    pad_starts = pad_ends - padded
    dest = pad_starts[idx] + rank
    n_blocks = -(-n * TOP_K // tmb) + n_exp
    block_expert = jnp.minimum(jnp.searchsorted(pad_ends, jnp.arange(n_blocks) * tmb, side='right'),
                               n_exp - 1).astype(jnp.int32)
    n_used = (pad_ends[-1:] // tmb).astype(jnp.int32)
    tok = jnp.repeat(jnp.arange(n, dtype=jnp.int32), TOP_K)
    slot_tok = jnp.zeros((n_blocks * tmb,), jnp.int32).at[dest.reshape(-1)].set(tok)
    h_all = jnp.concatenate([hp, hs], 0)
    x_sorted = jnp.take(h_all, slot_tok, axis=0)
    y_slots = _experts(x_sorted, block_expert, n_used, w1, b1, w2, b2, layer, tmb)
    yg = jnp.take(y_slots, dest.reshape(-1), axis=0).reshape(n, TOP_K * d)
    xp = _combine(rows_p, xp, mod_p[5], yg[:n_p], gates_p, final_g)
    xs = _combine(rows_s, xs, mod_s[5], yg[n_p:], gates_s, final_g)
    return xp, xs


def _proj_res_kernel(a_ref, w_ref, x_ref, gt_ref, o_ref):
    o_ref[...] = x_ref[...] + gt_ref[...] * jnp.dot(a_ref[...], w_ref[...], preferred_element_type=F32)


def _proj_residual(rows, a, w, x, gate):
    d = rows.d
    return pl.pallas_call(
        _proj_res_kernel,
        grid=(rows.grid,),
        in_specs=[rows.x_spec(), _const_spec((d, d)), rows.x_spec(), rows.mod_spec()],
        out_specs=rows.x_spec(),
        out_shape=jax.ShapeDtypeStruct((rows.n, d), F32),
        compiler_params=_params("arbitrary"),
        name="proj_residual",
    )(a, w.astype(BF16), x, rows.mod(gate))


def _nt_dot(a, b, precision=None):
    return lax.dot_general(a, b, (((1,), (1,)), ((), ())), precision=precision, preferred_element_type=F32)


def _fox_proj_kernel(seq_tiles, group, x_ref, g_ref, sh_ref, sc_ref, wqkv_ref, wf_ref, bf_ref,
                     q_ref, k_ref, v_ref, lf_ref, cum_ref, carry_ref):
    i = pl.program_id(0)
    tm, d = x_ref.shape
    h = _norm_mod(x_ref[...], g_ref[...], sh_ref[...], sc_ref[...])
    qkv = _bdot(h, wqkv_ref[...])
    q_ref[...] = qkv[:, :d]
    k_ref[...] = qkv[:, d:2 * d]
    v_ref[...] = qkv[:, 2 * d:]
    z = jnp.dot(h, wf_ref[...], precision=HIGHEST, preferred_element_type=F32) + bf_ref[...]
    lf = jnp.minimum(z, 0.0) - jnp.log1p(jnp.exp(-jnp.abs(z)))
    lf_ref[...] = lf
    r = lax.broadcasted_iota(jnp.int32, (tm, tm), 0)
    c = lax.broadcasted_iota(jnp.int32, (tm, tm), 1)
    if seq_tiles:
        @pl.when(i % seq_tiles == 0)
        def _():
            carry_ref[...] = jnp.zeros_like(carry_ref)

        tri = jnp.where(c <= r, 1.0, 0.0)
        cum_ref[...] = jnp.dot(tri, lf, precision=HIGHEST, preferred_element_type=F32) + carry_ref[...]
        carry_ref[...] = carry_ref[...] + jnp.sum(lf, axis=0, keepdims=True)
    else:
        tri = jnp.where(jnp.logical_and(c <= r, r // group == c // group), 1.0, 0.0)
        cum_ref[...] = jnp.dot(tri, lf, precision=HIGHEST, preferred_element_type=F32)


def _fox_proj(rows, x, g, shift, scale, w_qkv, w_f, b_f):
    d = rows.d
    nh = w_f.shape[1]
    wf = jnp.pad(w_f, ((0, 0), (0, LANES - nh)))
    bf = jnp.pad(b_f, (0, LANES - nh)).reshape(1, LANES)
    seq_tiles = rows.seq_len // rows.tm if rows.per_seq else 0
    lane_spec = pl.BlockSpec((rows.tm, LANES), lambda i: (i, 0))
    return pl.pallas_call(
        functools.partial(_fox_proj_kernel, seq_tiles, rows.seq_len),
        grid=(rows.grid,),
        in_specs=[rows.x_spec(), _const_spec((1, d)), rows.mod_spec(), rows.mod_spec(),
                  _const_spec((d, 3 * d)), _const_spec((d, LANES)), _const_spec((1, LANES))],
        out_specs=[rows.x_spec(), rows.x_spec(), rows.x_spec(), lane_spec, lane_spec],
        out_shape=[jax.ShapeDtypeStruct((rows.n, d), F32)] * 3 + [jax.ShapeDtypeStruct((rows.n, LANES), F32)] * 2,
        scratch_shapes=[pltpu.VMEM((1, LANES), F32)],
        compiler_params=_params("arbitrary"),
        name="fox_proj",
    )(x, g.reshape(1, d), rows.mod(shift), rows.mod(scale), w_qkv.astype(BF16), wf, bf)


def _flash_kernel(qi_ref, ki_ref, q_ref, k_ref, v_ref, cq_ref, ck_ref, o_ref, m_ref, l_ref, acc_ref):
    p_id = pl.program_id(1)
    qi = qi_ref[p_id]
    ki = ki_ref[p_id]
    tq = q_ref.shape[1]
    tk = k_ref.shape[1]
    nh = ck_ref.shape[1]
    hd = q_ref.shape[2] // nh
    scale = hd ** -0.5

    @pl.when(ki == 0)
    def _():
        m_ref[...] = jnp.full(m_ref.shape, -jnp.inf, F32)
        l_ref[...] = jnp.zeros(l_ref.shape, F32)
        acc_ref[...] = jnp.zeros(acc_ref.shape, F32)

    def body(masked):
        if masked:
            row = lax.broadcasted_iota(jnp.int32, (tq, tk), 0)
            col = lax.broadcasted_iota(jnp.int32, (tq, tk), 1)
            keep = col <= row
        for h in range(nh):
            sl = slice(h * hd, (h + 1) * hd)
            qh = (q_ref[0, :, sl] * scale).astype(BF16)
            kh = k_ref[0, :, sl].astype(BF16)
            s = _nt_dot(qh, kh) + (cq_ref[0, :, h:h + 1] - ck_ref[0, h:h + 1, :])
            if masked:
                s = jnp.where(keep, s, -jnp.inf)
            m_prev = m_ref[h]
            m_new = jnp.maximum(m_prev, jnp.max(s, axis=-1, keepdims=True))
            corr = jnp.exp(m_prev - m_new)
            p = jnp.exp(s - m_new)
            l_ref[h] = l_ref[h] * corr + jnp.sum(p, axis=-1, keepdims=True)
            acc_ref[:, sl] = acc_ref[:, sl] * corr + jnp.dot(p.astype(BF16), v_ref[0, :, sl].astype(BF16),
                                                            preferred_element_type=F32)
            m_ref[h] = m_new

    @pl.when(ki < qi)
    def _():
        body(False)

    @pl.when(ki == qi)
    def _():
        body(True)
        for h in range(nh):
            sl = slice(h * hd, (h + 1) * hd)
            o_ref[0, :, sl] = (acc_ref[:, sl] / l_ref[h]).astype(o_ref.dtype)


def _fox_flash(q, k, v, cum, cum_t, tile):
    b, t, d = q.shape
    nh = cum_t.shape[1]
    tq = _tile(t, tile)
    nq = t // tq
    pairs = [(a, c) for a in range(nq) for c in range(a + 1)]
    qi = jnp.array([p[0] for p in pairs], jnp.int32)
    ki = jnp.array([p[1] for p in pairs], jnp.int32)
    grid_spec = pltpu.PrefetchScalarGridSpec(
        num_scalar_prefetch=2,
        grid=(b, len(pairs)),
        in_specs=[
            pl.BlockSpec((1, tq, d), lambda bi, p, qi, ki: (bi, qi[p], 0)),
            pl.BlockSpec((1, tq, d), lambda bi, p, qi, ki: (bi, ki[p], 0)),
            pl.BlockSpec((1, tq, d), lambda bi, p, qi, ki: (bi, ki[p], 0)),
            pl.BlockSpec((1, tq, LANES), lambda bi, p, qi, ki: (bi, qi[p], 0)),
            pl.BlockSpec((1, nh, tq), lambda bi, p, qi, ki: (bi, 0, ki[p])),
        ],
        out_specs=pl.BlockSpec((1, tq, d), lambda bi, p, qi, ki: (bi, qi[p], 0)),
        scratch_shapes=[pltpu.VMEM((nh, tq, 1), F32), pltpu.VMEM((nh, tq, 1), F32), pltpu.VMEM((tq, d), F32)],
    )
    return pl.pallas_call(
        _flash_kernel,
        grid_spec=grid_spec,
        out_shape=jax.ShapeDtypeStruct((b, t, d), BF16),
        compiler_params=_params("arbitrary", "arbitrary"),
        name="fox_flash",
    )(qi, ki, q, k, v, cum, cum_t)


def _paged_kernel(n_pages, nh, pt_ref, q_ref, kn_ref, vn_ref, g_ref, kc_ref, vc_ref, lfc_ref, o_ref,
                  qbd_ref, acc_ref, m_ref, l_ref, carry_ref, lfp_ref):
    s_id = pl.program_id(1)
    dseq, d = q_ref.shape
    page = kc_ref.shape[1]
    rws = nh * dseq
    hd = d // nh
    row_h = lax.broadcasted_iota(jnp.int32, (rws, LANES), 0) // dseq
    lane = lax.broadcasted_iota(jnp.int32, (rws, LANES), 1)
    head_sel = jnp.where(lane == row_h, 1.0, 0.0)
    g_new = g_ref[...]
    gq = jnp.sum(head_sel * jnp.concatenate([g_new] * nh, axis=0), axis=-1, keepdims=True)

    def online(s, v_bf):
        m_prev = m_ref[...]
        m_new = jnp.maximum(m_prev, jnp.max(s, axis=-1, keepdims=True))
        corr = jnp.exp(m_prev - m_new)
        p = jnp.exp(s - m_new)
        l_ref[...] = l_ref[...] * corr + jnp.sum(p, axis=-1, keepdims=True)
        acc_ref[...] = acc_ref[...] * corr + jnp.dot(p.astype(BF16), v_bf, preferred_element_type=F32)
        m_ref[...] = m_new

    @pl.when(s_id == 0)
    def _():
        on_head = (lax.broadcasted_iota(jnp.int32, (rws, d), 0) // dseq
                   == lax.broadcasted_iota(jnp.int32, (rws, d), 1) // hd)
        q_rows = jnp.concatenate([q_ref[...] * (hd ** -0.5)] * nh, axis=0)
        qbd_ref[...] = jnp.where(on_head, q_rows, 0.0).astype(BF16)
        m_ref[...] = jnp.full(m_ref.shape, -jnp.inf, F32)
        l_ref[...] = jnp.zeros(l_ref.shape, F32)
        acc_ref[...] = jnp.zeros(acc_ref.shape, F32)
        carry_ref[...] = jnp.zeros(carry_ref.shape, F32)
        lfp_ref[...] = jnp.zeros(lfp_ref.shape, F32)
        pad = jnp.zeros((page - dseq, d), F32)
        k_pad = jnp.concatenate([kn_ref[...], pad], axis=0).astype(BF16)
        v_pad = jnp.concatenate([vn_ref[...], pad], axis=0).astype(BF16)
        g_pad = jnp.concatenate([g_new, jnp.zeros((page - dseq, LANES), F32)], axis=0)
        gk = _nt_dot(head_sel, g_pad, HIGHEST)
        t_row = lax.broadcasted_iota(jnp.int32, (rws, page), 0) % dseq
        u_col = lax.broadcasted_iota(jnp.int32, (rws, page), 1)
        s = _nt_dot(qbd_ref[...], k_pad) + (gq - gk)
        s = jnp.where(u_col <= t_row, s, -jnp.inf)
        online(s, v_pad)

    @pl.when(s_id > 0)
    def _():
        lfp_ref[:, 0:nh] = lfc_ref[0]
        lfp = lfp_ref[...]
        pj = lax.broadcasted_iota(jnp.int32, (page, page), 0)
        jj = lax.broadcasted_iota(jnp.int32, (page, page), 1)
        later = jnp.where(jj > pj, 1.0, 0.0)
        tail_t = jnp.dot(later, lfp, precision=HIGHEST, preferred_element_type=F32) + carry_ref[...]
        bias = _nt_dot(head_sel, tail_t, HIGHEST) + gq
        carry_ref[...] = carry_ref[...] + jnp.sum(lfp, axis=0, keepdims=True)
        s = _nt_dot(qbd_ref[...], kc_ref[0].astype(BF16)) + bias
        online(s, vc_ref[0].astype(BF16))

    @pl.when(s_id == n_pages)
    def _():
        on_head = (lax.broadcasted_iota(jnp.int32, (rws, d), 0) // dseq
                   == lax.broadcasted_iota(jnp.int32, (rws, d), 1) // hd)
        o_full = jnp.where(on_head, acc_ref[...] / l_ref[...], 0.0)
        o_ref[...] = jnp.sum(o_full.reshape(nh, dseq, d), axis=0).astype(o_ref.dtype)


def _fox_paged(q, k_new, v_new, cum_new, cache_k, cache_v, cache_lf, page_table, dseq):
    ns, d = q.shape
    bsz, n_pages = page_table.shape
    pool, page, nh = cache_lf.shape
    rws = nh * dseq

    def page_idx(bi, s, pt):
        return pt[bi * n_pages + n_pages - jnp.maximum(s, 1)]

    seq_spec = pl.BlockSpec((dseq, d), lambda bi, s, pt: (bi, 0))
    grid_spec = pltpu.PrefetchScalarGridSpec(
        num_scalar_prefetch=1,
        grid=(bsz, n_pages + 1),
        in_specs=[
            seq_spec, seq_spec, seq_spec,
            pl.BlockSpec((dseq, LANES), lambda bi, s, pt: (bi, 0)),
            pl.BlockSpec((1, page, d), lambda bi, s, pt: (page_idx(bi, s, pt), 0, 0)),
            pl.BlockSpec((1, page, d), lambda bi, s, pt: (page_idx(bi, s, pt), 0, 0)),
            pl.BlockSpec((1, page, nh), lambda bi, s, pt: (page_idx(bi, s, pt), 0, 0)),
        ],
        out_specs=seq_spec,
        scratch_shapes=[pltpu.VMEM((rws, d), BF16), pltpu.VMEM((rws, d), F32), pltpu.VMEM((rws, 1), F32),
                        pltpu.VMEM((rws, 1), F32), pltpu.VMEM((1, LANES), F32), pltpu.VMEM((page, LANES), F32)],
    )
    return pl.pallas_call(
        functools.partial(_paged_kernel, n_pages, nh),
        grid_spec=grid_spec,
        out_shape=jax.ShapeDtypeStruct((ns, d), BF16),
        compiler_params=_params("arbitrary", "arbitrary"),
        name="fox_paged",
    )(page_table.reshape(-1), q, k_new, v_new, cum_new,
      cache_k.reshape(pool, page, d), cache_v.reshape(pool, page, d), cache_lf)


def _fox_layer(rows_p, rows_s, xp, xs, g, mod_p, mod_s, cache_k, cache_v, cache_lf, page_table,
               w_qkv, w_f, b_f, w_o, flash_tile):
    d = rows_p.d
    nh = w_f.shape[1]
    bp, t = rows_p.n_seq, rows_p.seq_len
    qp, kp, vp, lfp, cump = _fox_proj(rows_p, xp, g, mod_p[0], mod_p[1], w_qkv, w_f, b_f)
    qs, ks, vs, lfs, cums = _fox_proj(rows_s, xs, g, mod_s[0], mod_s[1], w_qkv, w_f, b_f)
    cum3 = cump.reshape(bp, t, LANES)
    cum_t = jnp.transpose(cum3[:, :, :nh], (0, 2, 1))
    op = _fox_flash(qp.reshape(bp, t, d), kp.reshape(bp, t, d), vp.reshape(bp, t, d), cum3, cum_t, flash_tile)
    os_ = _fox_paged(qs, ks, vs, cums, cache_k, cache_v, cache_lf, page_table, rows_s.seq_len)
    xp = _proj_residual(rows_p, op.reshape(-1, d), w_o, xp, mod_p[2])
    xs = _proj_residual(rows_s, os_, w_o, xs, mod_s[2])
    return xp, xs, (kp, vp, lfp[:, :nh]), (ks, vs, lfs[:, :nh])


def _log_sigmoid(z):
    return jnp.minimum(z, 0.0) - jnp.log1p(jnp.exp(-jnp.abs(z)))


def _head_sum(x, hd):
    i = lax.broadcasted_iota(jnp.int32, (LANES, LANES), 0) // hd
    j = lax.broadcasted_iota(jnp.int32, (LANES, LANES), 1) // hd
    same = jnp.where(i == j, 1.0, 0.0)
    parts = [jnp.dot(x[:, c * LANES:(c + 1) * LANES], same, precision=HIGHEST, preferred_element_type=F32)
             for c in range(x.shape[1] // LANES)]
    return jnp.concatenate(parts, axis=1)


def _rwkv_proj_kernel(seq_tiles, group, x_ref, xprev_ref, sh_ref, g_ref, shift_ref, scale_ref, mu_ref, wrkv_ref,
                      w0_ref, w1_ref, w2_ref, a0_ref, a1_ref, a2_ref, g1_ref, g2_ref, kk_ref, ka_ref, rk_ref,
                      r_out, lw_out, k_out, v_out, a_out, b_out, bonus_out, gate_out, h_out):
    i = pl.program_id(0)
    tm, d = x_ref.shape
    gvec, shift, scale = g_ref[...], shift_ref[...], scale_ref[...]
    h = _norm_mod(x_ref[...], gvec, shift, scale)
    rolled = pltpu.roll(h, 1, 0)
    row = lax.broadcasted_iota(jnp.int32, (tm, d), 0)
    if seq_tiles:
        h_before = _norm_mod(xprev_ref[...], gvec, shift, scale)[7:8]
        first = jnp.where(i % seq_tiles == 0, sh_ref[...], h_before)
        x_prev = jnp.where(row == 0, first, rolled)
        h_out[...] = h[tm - 8:]
    else:
        x_prev = jnp.where(row % group == 0, sh_ref[...], rolled)
        h_out[...] = h
    xx = x_prev - h
    xr, xw, xk, xv, xa, xg = (h + xx * mu_ref[n:n + 1] for n in range(6))
    r = _bdot(xr, wrkv_ref[0])
    k = _bdot(xk, wrkv_ref[1])
    v = _bdot(xv, wrkv_ref[2])
    w_log = _log_sigmoid(w0_ref[...] + _bdot(jnp.tanh(_bdot(xw, w1_ref[...])), w2_ref[...])) - 0.5
    iclr = jax.nn.sigmoid(a0_ref[...] + _bdot(_bdot(xa, a1_ref[...]), a2_ref[...]))
    gate_out[...] = _bdot(jax.nn.sigmoid(_bdot(xg, g1_ref[...])), g2_ref[...])
    kk = k * kk_ref[...]
    kk = kk / jnp.maximum(jnp.sqrt(_head_sum(kk * kk, RW_HEAD)), 1e-12)
    k2 = k * (1.0 + (iclr - 1.0) * ka_ref[...])
    r_out[...] = r
    lw_out[...] = -jnp.exp(w_log)
    k_out[...] = k2
    v_out[...] = v
    a_out[...] = -kk
    b_out[...] = kk * iclr
    bonus_out[...] = _head_sum(r * k2 * rk_ref[...], RW_HEAD) * v


def _pad_to(x, axis, size):
    pad = [(0, 0)] * x.ndim
    pad[axis] = (0, size - x.shape[axis])
    return jnp.pad(x, pad)


def _rwkv_proj(rows, x, g, shift, scale, shift_state, mu, w_rkv, w0, w1, w2, a0, a1, a2, g1, g2, k_k, k_a, r_k):
    d = rows.d
    lw_, la_, lg_ = (-(-w.shape[1] // LANES) * LANES for w in (w1, a1, g1))
    seq_tiles = rows.seq_len // rows.tm if rows.per_seq else 0
    if rows.per_seq:
        per8 = rows.tm // 8
        xprev_spec = pl.BlockSpec((8, d), lambda i: (jnp.maximum(i * per8 - 1, 0), 0))
        h_shape = jax.ShapeDtypeStruct((rows.grid * 8, d), F32)
        h_spec = pl.BlockSpec((8, d), lambda i: (i, 0))
    else:
        xprev_spec = pl.BlockSpec((8, d), lambda i: (0, 0))
        h_shape = jax.ShapeDtypeStruct((rows.n, d), F32)
        h_spec = rows.x_spec()
    vec = _const_spec((1, d))
    out = pl.pallas_call(
        functools.partial(_rwkv_proj_kernel, seq_tiles, rows.seq_len),
        grid=(rows.grid,),
        in_specs=[rows.x_spec(), xprev_spec, rows.mod_spec(), vec, rows.mod_spec(), rows.mod_spec(),
                  _const_spec((6, d)), _const_spec((3, d, d)),
                  vec, _const_spec((d, lw_)), _const_spec((lw_, d)),
                  vec, _const_spec((d, la_)), _const_spec((la_, d)),
                  _const_spec((d, lg_)), _const_spec((lg_, d)), vec, vec, vec],
        out_specs=[rows.x_spec()] * 8 + [h_spec],
        out_shape=[jax.ShapeDtypeStruct((rows.n, d), F32)] * 8 + [h_shape],
        compiler_params=_params("arbitrary"),
        name="rwkv_proj",
    )(x, x, rows.mod(shift_state), g.reshape(1, d), rows.mod(shift), rows.mod(scale), mu, w_rkv.astype(BF16),
      w0.reshape(1, d), _pad_to(w1, 1, lw_).astype(BF16), _pad_to(w2, 0, lw_).astype(BF16),
      a0.reshape(1, d), _pad_to(a1, 1, la_).astype(BF16), _pad_to(a2, 0, la_).astype(BF16),
      _pad_to(g1, 1, lg_).astype(BF16), _pad_to(g2, 0, lg_).astype(BF16),
      k_k.reshape(1, d), k_a.reshape(1, d), r_k.reshape(1, d))
    return out


def _tn_dot(a, b):
    return lax.dot_general(a.astype(BF16), b.astype(BF16), (((0,), (0,)), ((), ())), preferred_element_type=F32)


def _wkv_kernel(chunk, r_ref, lw_ref, k_ref, v_ref, a_ref, b_ref, s0_ref, y_ref, sT_ref, st_ref):
    tt_id = pl.program_id(2)
    tt = r_ref.shape[1]
    hd = RW_HEAD
    ln = chunk

    @pl.when(tt_id == 0)
    def _():
        st_ref[...] = s0_ref[0]

    rr = lax.broadcasted_iota(jnp.int32, (ln, ln), 0)
    cc = lax.broadcasted_iota(jnp.int32, (ln, ln), 1)
    incl = cc <= rr
    strict = cc < rr
    tri = jnp.where(incl, 1.0, 0.0)
    eye = jnp.where(cc == rr, 1.0, 0.0)
    n_double = ln.bit_length() - 2
    for c in range(tt // ln):
        rows = slice(c * ln, (c + 1) * ln)
        lw = lw_ref[0, rows, :]
        cw = jnp.dot(tri, lw, precision=HIGHEST, preferred_element_type=F32)
        e_neg = jnp.exp(-cw)
        a_t = a_ref[0, rows, :] * jnp.exp(cw - lw)
        r_t = r_ref[0, rows, :] * jnp.exp(cw)
        b_t = b_ref[0, rows, :] * e_neg
        k_t = k_ref[0, rows, :] * e_neg
        e_all = jnp.exp(cw[ln - 1:ln, :])
        b_end = b_t * e_all
        k_end = k_t * e_all
        vv = v_ref[0, rows, :]
        ys = []
        for j in range(2):
            sl = slice(j * hd, (j + 1) * hd)
            state = st_ref[j]
            lhs = jnp.concatenate([a_t[:, sl], r_t[:, sl]], axis=0).astype(BF16)
            rhs = jnp.concatenate([b_t[:, sl], k_t[:, sl]], axis=0).astype(BF16)
            gram = _nt_dot(lhs, rhs)
            a_ab = jnp.where(strict, gram[:ln, :ln], 0.0)
            a_ak = jnp.where(strict, gram[:ln, ln:], 0.0)
            m_b = jnp.where(incl, gram[ln:, :ln], 0.0)
            m_k = jnp.where(incl, gram[ln:, ln:], 0.0)
            inv = eye + a_ab
            power = a_ab
            for _ in range(n_double):
                power = _bdot(power, power)
                inv = inv + _bdot(inv, power)
            from_state = _nt_dot(lhs, state.astype(BF16))
            vj = vv[:, sl]
            u = _bdot(inv, from_state[:ln] + _bdot(a_ak, vj))
            ys.append(from_state[ln:] + _bdot(m_b, u) + _bdot(m_k, vj))
            st_ref[j] = state * e_all[:, sl] + _tn_dot(u, b_end[:, sl]) + _tn_dot(vj, k_end[:, sl])
        y_ref[0, rows, :] = jnp.concatenate(ys, axis=1)

    @pl.when(tt_id == pl.num_programs(2) - 1)
    def _():
        sT_ref[0] = st_ref[...]


def _wkv(r, lw, k, v, a, b, state0, chunk, tile):
    bsz, t, d = r.shape
    tt = _tile(t, tile)
    assert tt % chunk == 0
    seq_spec = pl.BlockSpec((1, tt, LANES), lambda bi, hp, ti: (bi, ti, hp))
    st_spec = pl.BlockSpec((1, 2, RW_HEAD, RW_HEAD), lambda bi, hp, ti: (bi, hp, 0, 0))
    return pl.pallas_call(
        functools.partial(_wkv_kernel, chunk),
        grid=(bsz, d // LANES, t // tt),
        in_specs=[seq_spec] * 6 + [st_spec],
        out_specs=[seq_spec, st_spec],
        out_shape=[jax.ShapeDtypeStruct((bsz, t, d), F32), jax.ShapeDtypeStruct(state0.shape, F32)],
        scratch_shapes=[pltpu.VMEM((2, RW_HEAD, RW_HEAD), F32)],
        compiler_params=_params("arbitrary", "arbitrary", "arbitrary"),
        name="wkv",
    )(r, lw, k, v, a, b, state0)


def _rwkv_out_kernel(y_ref, bonus_ref, gate_ref, x_ref, gt_ref, lng_ref, lnb_ref, wo_ref, o_ref):
    y = y_ref[...]
    mean = _head_sum(y, RW_HEAD) * (1.0 / RW_HEAD)
    yc = y - mean
    var = _head_sum(yc * yc, RW_HEAD) * (1.0 / RW_HEAD)
    yn = yc * lax.rsqrt(var + RW_GN_EPS) * lng_ref[...] + lnb_ref[...] + bonus_ref[...]
    o_ref[...] = x_ref[...] + gt_ref[...] * _bdot(yn * gate_ref[...], wo_ref[...])


def _rwkv_out(rows, y, bonus, gate, x, gate1, ln_g, ln_b, w_o):
    d = rows.d
    return pl.pallas_call(
        _rwkv_out_kernel,
        grid=(rows.grid,),
        in_specs=[rows.x_spec()] * 4 + [rows.mod_spec(), _const_spec((1, d)), _const_spec((1, d)), _const_spec((d, d))],
        out_specs=rows.x_spec(),
        out_shape=jax.ShapeDtypeStruct((rows.n, d), F32),
        compiler_params=_params("arbitrary"),
        name="rwkv_out",
    )(y, bonus, gate, x, rows.mod(gate1), ln_g.reshape(1, d), ln_b.reshape(1, d), w_o.astype(BF16))


def _rwkv_group(rows, x, g, mod, shift_state, wkv_state, mu, w_rkv, w_o, w0, w1, w2, a0, a1, a2, g1, g2,
                k_k, k_a, r_k, ln_g, ln_b, chunk, tile):
    d = rows.d
    bsz, t = rows.n_seq, rows.seq_len
    r, lw, k, v, a, b, bonus, gate, h_tail = _rwkv_proj(rows, x, g, mod[0], mod[1], shift_state, mu, w_rkv,
                                                        w0, w1, w2, a0, a1, a2, g1, g2, k_k, k_a, r_k)
    to3 = lambda z: z.reshape(bsz, t, d)
    y, state = _wkv(to3(r), to3(lw), to3(k), to3(v), to3(a), to3(b), wkv_state, chunk, tile)
    x_new = _rwkv_out(rows, y.reshape(-1, d), bonus, gate, x, mod[2], ln_g, ln_b, w_o)
    if rows.per_seq:
        h_last = h_tail.reshape(bsz, -1, d)[:, -1]
    else:
        h_last = h_tail.reshape(bsz, t, d)[:, -1]
    return x_new, h_last, state


def kernel(x_prompt, x_sample, cache_fox_k, cache_fox_v, cache_fox_logf, state_rwkv_shift, state_rwkv_wkv, page_table, c_prompt, c_sample, norm1_g, norm2_g, ada_w, ada_b, cm_w_in, cm_b_in, cm_ln_g, cm_ln_b, cm_w_s, cm_b_s, cm_w_out, cm_b_out, rw_mu, rw_w_rkv, rw_w_o, rw_w0, rw_w1, rw_w2, rw_a0, rw_a1, rw_a2, rw_g1, rw_g2, rw_k_k, rw_k_a, rw_r_k, rw_ln_g, rw_ln_b, fox_w_qkv, fox_w_f, fox_b_f, fox_w_o, moe_wr, moe_br, moe_w1, moe_b1, moe_w2, moe_b2, final_g):
    bp, seq, d = x_prompt.shape
    bs, dseq, _ = x_sample.shape
    c_all = jnp.concatenate([c_prompt, c_sample], 0)
    n = c_all.shape[0]
    npad = -(-n // 8) * 8
    mod = _adaln(jnp.pad(c_all, ((0, npad - n), (0, 0))), ada_w, ada_b)
    depth = ada_w.shape[0]
    rows_p = _Rows(bp, seq, d, ROW_TILE)
    rows_s = _Rows(bs, dseq, d, ROW_TILE)
    width = cm_w_out.shape[1]
    nh_rw = d // RW_HEAD
    nh_fox = fox_w_f.shape[2]
    xp = x_prompt.reshape(-1, d)
    xs = x_sample.reshape(-1, d)
    cm_v_s = []
    rw_sh_p, rw_wkv_p, rw_sh_s, rw_wkv_s = [], [], [], []
    fx_p, fx_s = [], []
    for i in range(depth):
        kind, j = i % N_MIXERS, i // N_MIXERS
        mp = jnp.split(mod[i, :bp], 6, axis=-1)
        ms = jnp.split(mod[i, bp:n], 6, axis=-1)
        if kind == 0:
            cm = (cm_w_in[j], cm_b_in[j], cm_ln_g[j], cm_ln_b[j])
            ws_p, bs_p = _cm_spatial_weights(cm_w_s[j], cm_b_s[j], CHUNK, width)
            ws_s, bs_s = _cm_spatial_weights(cm_w_s[j], cm_b_s[j], dseq, width)
            (xp,) = _cm_layer(rows_p, xp, norm1_g[i], mp[0], mp[1], mp[2], *cm, ws_p, bs_p, cm_w_out[j], cm_b_out[j], False)
            xs, v_new = _cm_layer(rows_s, xs, norm1_g[i], ms[0], ms[1], ms[2], *cm, ws_s, bs_s, cm_w_out[j], cm_b_out[j], True)
            cm_v_s.append(v_new.reshape(bs, dseq, width))
        elif kind == 1:
            rw = (rw_mu[j], rw_w_rkv[j], rw_w_o[j], rw_w0[j], rw_w1[j], rw_w2[j], rw_a0[j], rw_a1[j], rw_a2[j],
                  rw_g1[j], rw_g2[j], rw_k_k[j], rw_k_a[j], rw_r_k[j], rw_ln_g[j], rw_ln_b[j])
            xp, sh_p, wkv_p = _rwkv_group(rows_p, xp, norm1_g[i], mp, jnp.zeros((bp, d), F32),
                                          jnp.zeros((bp, nh_rw, RW_HEAD, RW_HEAD), F32), *rw,
                                          min(WKV_CHUNK, seq), WKV_TILE)
            xs, sh_s, wkv_s = _rwkv_group(rows_s, xs, norm1_g[i], ms, state_rwkv_shift[j], state_rwkv_wkv[j], *rw,
                                          dseq, dseq)
            rw_sh_p.append(sh_p)
            rw_wkv_p.append(wkv_p)
            rw_sh_s.append(sh_s)
            rw_wkv_s.append(wkv_s)
        else:
            xp, xs, (k_p, v_p, lf_p), (k_s, v_s, lf_s) = _fox_layer(
                rows_p, rows_s, xp, xs, norm1_g[i], mp, ms, cache_fox_k[j], cache_fox_v[j], cache_fox_logf[j],
                page_table, fox_w_qkv[j], fox_w_f[j], fox_b_f[j], fox_w_o[j], FLASH_TILE)
            hd = d // nh_fox
            fx_p.append((k_p.reshape(bp, seq, nh_fox, hd), v_p.reshape(bp, seq, nh_fox, hd),
                         lf_p.reshape(bp, seq, nh_fox)))
            fx_s.append((k_s.reshape(bs, dseq, nh_fox, hd), v_s.reshape(bs, dseq, nh_fox, hd),
                         lf_s.reshape(bs, dseq, nh_fox)))
        xp, xs = _moe_layer(i, rows_p, rows_s, xp, xs, norm2_g[i], mp, ms, moe_wr[i], moe_br[i],
                            moe_w1, moe_b1, moe_w2, moe_b2, final_g if i == depth - 1 else None, MOE_ROWS)
    return (xp.reshape(bp, seq, d), xs.reshape(bs, dseq, d), jnp.stack(cm_v_s),
            jnp.stack(rw_sh_p), jnp.stack(rw_wkv_p), jnp.stack(rw_sh_s), jnp.stack(rw_wkv_s),
            jnp.stack([f[0] for f in fx_p]), jnp.stack([f[1] for f in fx_p]), jnp.stack([f[2] for f in fx_p]),
            jnp.stack([f[0] for f in fx_s]), jnp.stack([f[1] for f in fx_s]), jnp.stack([f[2] for f in fx_s]))
```

```python
import functools

import jax
import jax.numpy as jnp
from jax import lax
from jax.experimental import pallas as pl
from jax.experimental.pallas import tpu as pltpu

F32 = jnp.float32
BF16 = jnp.bfloat16
HIGHEST = lax.Precision.HIGHEST

NORM_EPS = 1e-6
N_MIXERS = 3
CHUNK = 128
CM_GROUPS = 8
RW_HEAD = 64
RW_GN_EPS = 64e-5
FOX_HEAD = 64
PAGE_SIZE = 128
TOP_K = 4
SWIGLU_LIMIT = 7.0
SWIGLU_ALPHA = 1.702
MOE_BLOCK = 128

V7X_VMEM_BYTES = 64 * 1024 * 1024
VMEM_LIMIT = V7X_VMEM_BYTES - 8 * 1024 * 1024
LANES = 128

ROW_TILE = 256
FLASH_TILE = 512
WKV_CHUNK = 64
WKV_TILE = 256
WKV_LANES = 256
WKV_DECODE_SEQS = 8
MOE_ROWS = 256


def _params(*sem):
    return pltpu.CompilerParams(dimension_semantics=sem, vmem_limit_bytes=VMEM_LIMIT)


def _bdot(a, b):
    return jnp.dot(a.astype(BF16), b.astype(BF16), preferred_element_type=F32)


def _norm_mod(x, g, shift, scale):
    y = x * lax.rsqrt(jnp.mean(x * x, axis=-1, keepdims=True) + NORM_EPS)
    return (y * g) * (1.0 + scale) + shift


def _tile(n, want):
    t = min(n, want)
    assert n % t == 0, (n, want)
    return t


class _Rows:
    def __init__(self, n_seq, seq_len, d, want_tile):
        self.n_seq, self.seq_len, self.d = n_seq, seq_len, d
        self.n = n_seq * seq_len
        self.per_seq = seq_len >= want_tile or seq_len >= 128
        if self.per_seq:
            self.tm = _tile(seq_len, want_tile)
        else:
            self.tm = _tile(self.n, want_tile)
            assert self.tm % seq_len == 0
        self.grid = self.n // self.tm

    def x_spec(self, width=None):
        return pl.BlockSpec((self.tm, width or self.d), lambda i: (i, 0))

    def mod_spec(self):
        if self.per_seq:
            per = self.seq_len // self.tm
            return pl.BlockSpec((None, 1, self.d), lambda i: (i // per, 0, 0))
        return pl.BlockSpec((self.tm, self.d), lambda i: (i, 0))

    def mod(self, m):
        if self.per_seq:
            return m.reshape(self.n_seq, 1, self.d)
        return jnp.repeat(m, self.seq_len, axis=0)


def _const_spec(shape):
    nd = len(shape)
    return pl.BlockSpec(shape, lambda *_: (0,) * nd)


def _adaln_kernel(c_ref, w_ref, b_ref, o_ref):
    c = c_ref[...]
    s = c * jax.nn.sigmoid(c)
    o_ref[0] = _bdot(s, w_ref[0]) + b_ref[0]


def _adaln(c_all, ada_w, ada_b):
    depth, d, d6 = ada_w.shape
    n = c_all.shape[0]
    tn = _tile(d6, 1536)
    return pl.pallas_call(
        _adaln_kernel,
        grid=(depth, d6 // tn),
        in_specs=[
            pl.BlockSpec((n, d), lambda i, j: (0, 0)),
            pl.BlockSpec((1, d, tn), lambda i, j: (i, 0, j)),
            pl.BlockSpec((1, 1, tn), lambda i, j: (i, 0, j)),
        ],
        out_specs=pl.BlockSpec((1, n, tn), lambda i, j: (i, 0, j)),
        out_shape=jax.ShapeDtypeStruct((depth, n, d6), F32),
        compiler_params=_params("arbitrary", "arbitrary"),
        name="adaln",
    )(c_all, ada_w, ada_b.reshape(depth, 1, d6))


def _cm_kernel(emit_v, x_ref, g_ref, sh_ref, sc_ref, gt_ref, win_ref, bin_ref, lng_ref, lnb_ref, ws_ref, bs_ref,
               wout_ref, bout_ref, xo_ref, *rest):
    if emit_v:
        v_ref, gated_ref = rest
    else:
        (gated_ref,) = rest
    tm = x_ref.shape[0]
    width = lng_ref.shape[-1]
    gd = width // CM_GROUPS
    x = x_ref[...]
    h = _norm_mod(x, g_ref[...], sh_ref[...], sc_ref[...])
    z = jax.nn.gelu(_bdot(h, win_ref[...]) + bin_ref[...])
    u = z[:, :width]
    v = z[:, width:]
    mu = jnp.mean(v, axis=-1, keepdims=True)
    vc = v - mu
    var = jnp.mean(vc * vc, axis=-1, keepdims=True)
    vn = vc * lax.rsqrt(var + NORM_EPS) * lng_ref[...] + lnb_ref[...]
    if emit_v:
        v_ref[...] = vn
    vb = vn.astype(BF16)
    row = lax.broadcasted_iota(jnp.int32, (CHUNK, CHUNK), 0)
    col = lax.broadcasted_iota(jnp.int32, (CHUNK, CHUNK), 1)
    causal = col <= row
    for g in range(CM_GROUPS):
        wg = jnp.where(causal, ws_ref[g], 0.0).astype(BF16)
        for c in range(tm // CHUNK):
            r0 = c * CHUNK
            s = jnp.dot(wg, vb[r0:r0 + CHUNK, g * gd:(g + 1) * gd], preferred_element_type=F32)
            s = s + bs_ref[:, g * gd:(g + 1) * gd]
            gated_ref[r0:r0 + CHUNK, g * gd:(g + 1) * gd] = (u[r0:r0 + CHUNK, g * gd:(g + 1) * gd] * s).astype(BF16)
    y = jnp.dot(gated_ref[...], wout_ref[...], preferred_element_type=F32) + bout_ref[...]
    xo_ref[...] = x + gt_ref[...] * y


def _cm_layer(rows, x, g, shift, scale, gate, w_in, b_in, ln_g, ln_b, ws_bd, bs_rows, w_out, b_out, emit_v):
    d = rows.d
    width = w_out.shape[0]
    tm = rows.tm
    assert tm % CHUNK == 0
    out_shape = [jax.ShapeDtypeStruct((rows.n, d), F32)]
    out_specs = [rows.x_spec()]
    if emit_v:
        out_shape.append(jax.ShapeDtypeStruct((rows.n, width), F32))
        out_specs.append(rows.x_spec(width))
    res = pl.pallas_call(
        functools.partial(_cm_kernel, emit_v),
        grid=(rows.grid,),
        in_specs=[
            rows.x_spec(), _const_spec((1, d)), rows.mod_spec(), rows.mod_spec(), rows.mod_spec(),
            _const_spec((d, 2 * width)), _const_spec((1, 2 * width)), _const_spec((1, width)), _const_spec((1, width)),
            _const_spec((CM_GROUPS, CHUNK, CHUNK)), _const_spec((CHUNK, width)),
            _const_spec((width, d)), _const_spec((1, d)),
        ],
        out_specs=out_specs,
        out_shape=out_shape,
        scratch_shapes=[pltpu.VMEM((tm, width), BF16)],
        compiler_params=_params("arbitrary"),
        name="cm_layer_v" if emit_v else "cm_layer",
    )(x, g.reshape(1, d), rows.mod(shift), rows.mod(scale), rows.mod(gate),
      w_in.astype(BF16), b_in.reshape(1, -1), ln_g.reshape(1, -1), ln_b.reshape(1, -1),
      ws_bd, bs_rows, w_out.astype(BF16), b_out.reshape(1, d))
    return res


def _cm_spatial_weights(w_s, b_s, chunk_len, width):
    groups = w_s.shape[0]
    rep = CHUNK // chunk_len
    ws = w_s[:, :chunk_len, :chunk_len]
    if rep > 1:
        eye = jnp.eye(rep, dtype=w_s.dtype)
        ws = jnp.einsum("ab,grq->garbq", eye, ws).reshape(groups, CHUNK, CHUNK)
    bs = jnp.tile(b_s[:, :chunk_len].T, (rep, 1))
    return ws, jnp.repeat(bs, width // groups, axis=1)


NEG_BIG = -1e30


def _router_kernel(x_ref, g_ref, sh_ref, sc_ref, wr_ref, br_ref, cin_ref,
                   h_ref, idx_ref, gate_ref, rank_ref, cnt_ref):
    i = pl.program_id(0)
    tm = x_ref.shape[0]

    @pl.when(i == 0)
    def _():
        cnt_ref[...] = cin_ref[...]

    h = _norm_mod(x_ref[...], g_ref[...], sh_ref[...], sc_ref[...])
    h_ref[...] = h
    logits = jnp.dot(h, wr_ref[...], precision=HIGHEST, preferred_element_type=F32) + br_ref[...]
    col = lax.broadcasted_iota(jnp.int32, (tm, LANES), 1).astype(F32)
    work = logits
    hits = jnp.zeros((tm, LANES), F32)
    vals, sels = [], []
    for _ in range(TOP_K):
        m = jnp.max(work, axis=-1, keepdims=True)
        sel = jnp.min(jnp.where(work == m, col, float(LANES)), axis=-1, keepdims=True)
        hit = col == sel
        hits = hits + hit.astype(F32)
        work = jnp.where(hit, -jnp.inf, work)
        vals.append(m)
        sels.append(sel)
    es = [jnp.exp(v - vals[0]) for v in vals]
    tot = es[0] + es[1] + es[2] + es[3]
    r = lax.broadcasted_iota(jnp.int32, (tm, tm), 0)
    c = lax.broadcasted_iota(jnp.int32, (tm, tm), 1)
    below = jnp.where(c < r, 1.0, 0.0).astype(BF16)
    prefix = jnp.dot(below, hits.astype(BF16), preferred_element_type=F32) + cnt_ref[...]
    idx_o = jnp.zeros((tm, LANES), F32)
    gate_o = jnp.zeros((tm, LANES), F32)
    rank_o = jnp.zeros((tm, LANES), F32)
    for k in range(TOP_K):
        rank_k = jnp.sum(jnp.where(col == sels[k], prefix, 0.0), axis=-1, keepdims=True)
        idx_o = jnp.where(col == float(k), sels[k], idx_o)
        gate_o = jnp.where(col == float(k), es[k] / tot, gate_o)
        rank_o = jnp.where(col == float(k), rank_k, rank_o)
    idx_ref[...] = idx_o.astype(jnp.int32)
    gate_ref[...] = gate_o
    rank_ref[...] = rank_o.astype(jnp.int32)
    cnt_ref[...] = cnt_ref[...] + jnp.sum(hits, axis=0, keepdims=True)


def _router(rows, x, g, shift, scale, w_r, b_r, counts_in):
    d = rows.d
    n_exp = w_r.shape[1]
    wr = jnp.pad(w_r, ((0, 0), (0, LANES - n_exp)))
    br = jnp.pad(b_r, (0, LANES - n_exp), constant_values=NEG_BIG).reshape(1, LANES)
    lane_spec = pl.BlockSpec((rows.tm, LANES), lambda i: (i, 0))
    return pl.pallas_call(
        _router_kernel,
        grid=(rows.grid,),
        in_specs=[rows.x_spec(), _const_spec((1, d)), rows.mod_spec(), rows.mod_spec(),
                  _const_spec((d, LANES)), _const_spec((1, LANES)), _const_spec((1, LANES))],
        out_specs=[rows.x_spec(), lane_spec, lane_spec, lane_spec, _const_spec((1, LANES))],
        out_shape=[jax.ShapeDtypeStruct((rows.n, d), F32),
                   jax.ShapeDtypeStruct((rows.n, LANES), jnp.int32),
                   jax.ShapeDtypeStruct((rows.n, LANES), F32),
                   jax.ShapeDtypeStruct((rows.n, LANES), jnp.int32),
                   jax.ShapeDtypeStruct((1, LANES), F32)],
        compiler_params=_params("arbitrary"),
        name="moe_router",
    )(x, g.reshape(1, d), rows.mod(shift), rows.mod(scale), wr, br, counts_in)


def _experts_kernel(be_ref, nu_ref, x_ref, w1_ref, b1_ref, w2_ref, b2_ref, y_ref, w1b_ref, w2b_ref):
    i = pl.program_id(0)
    f = w2_ref.shape[2]
    new_expert = jnp.logical_or(i == 0, be_ref[i] != be_ref[jnp.maximum(i - 1, 0)])

    @pl.when(jnp.logical_and(new_expert, i < nu_ref[0]))
    def _():
        w1b_ref[...] = w1_ref[0, 0].astype(BF16)
        w2b_ref[...] = w2_ref[0, 0].astype(BF16)

    @pl.when(i < nu_ref[0])
    def _():
        hcat = jnp.dot(x_ref[...].astype(BF16), w1b_ref[...], preferred_element_type=F32) + b1_ref[0, 0]
        gate = jnp.minimum(hcat[:, :f], SWIGLU_LIMIT)
        up = jnp.clip(hcat[:, f:], -SWIGLU_LIMIT, SWIGLU_LIMIT)
        act = (up + 1.0) * (gate * jax.nn.sigmoid(SWIGLU_ALPHA * gate))
        y_ref[...] = jnp.dot(act.astype(BF16), w2b_ref[...], preferred_element_type=F32) + b2_ref[0, 0]

    @pl.when(i >= nu_ref[0])
    def _():
        y_ref[...] = jnp.zeros(y_ref.shape, F32)


def _experts(x_sorted, block_expert, n_used, w1, b1, w2, b2, layer, tmb):
    n_slots, d = x_sorted.shape
    _, n_exp, _, f2 = w1.shape
    f = f2 // 2
    n_blocks = n_slots // tmb
    grid_spec = pltpu.PrefetchScalarGridSpec(
        num_scalar_prefetch=2,
        grid=(n_blocks,),
        in_specs=[
            pl.BlockSpec((tmb, d), lambda i, be, nu: (i, 0)),
            pl.BlockSpec((1, 1, d, f2), lambda i, be, nu: (layer, be[i], 0, 0)),
            pl.BlockSpec((1, 1, 1, f2), lambda i, be, nu: (layer, be[i], 0, 0)),
            pl.BlockSpec((1, 1, f, d), lambda i, be, nu: (layer, be[i], 0, 0)),
            pl.BlockSpec((1, 1, 1, d), lambda i, be, nu: (layer, be[i], 0, 0)),
        ],
        out_specs=pl.BlockSpec((tmb, d), lambda i, be, nu: (i, 0)),
        scratch_shapes=[pltpu.VMEM((d, f2), BF16), pltpu.VMEM((f, d), BF16)],
    )
    return pl.pallas_call(
        _experts_kernel,
        grid_spec=grid_spec,
        out_shape=jax.ShapeDtypeStruct((n_slots, d), F32),
        compiler_params=_params("arbitrary"),
        name="moe_experts",
    )(block_expert, n_used, x_sorted, w1, b1.reshape(b1.shape[0], n_exp, 1, f2), w2, b2.reshape(b2.shape[0], n_exp, 1, d))


ROW_DMA_UNROLL = 8


def _dispatch_kernel(dest_ref, h_ref, xs_in_ref, xs_ref, sem):
    del xs_in_ref
    tm = h_ref.shape[0]

    def row_copy(t, slot):
        return pltpu.make_async_copy(h_ref.at[pl.ds(t, 1)], xs_ref.at[pl.ds(slot, 1)], sem)

    def start(t, carry):
        for k in range(TOP_K):
            row_copy(t, dest_ref[t * TOP_K + k]).start()
        return carry

    def wait(t, carry):
        for k in range(TOP_K):
            row_copy(t, dest_ref[t * TOP_K + k]).wait()
        return carry

    lax.fori_loop(0, tm, start, 0, unroll=ROW_DMA_UNROLL)
    lax.fori_loop(0, tm, wait, 0, unroll=ROW_DMA_UNROLL)


def _dispatch(rows, h, dest_flat, x_sorted):
    d = rows.d
    tm = rows.tm
    return pl.pallas_call(
        _dispatch_kernel,
        grid=(rows.grid,),
        in_specs=[pl.BlockSpec((tm * TOP_K,), lambda i: (i,), memory_space=pltpu.SMEM),
                  rows.x_spec(), pl.BlockSpec(memory_space=pl.ANY)],
        out_specs=pl.BlockSpec(memory_space=pl.ANY),
        out_shape=jax.ShapeDtypeStruct(x_sorted.shape, x_sorted.dtype),
        scratch_shapes=[pltpu.SemaphoreType.DMA(())],
        input_output_aliases={2: 0},
        compiler_params=_params("arbitrary"),
        name="moe_dispatch",
    )(dest_flat, h, x_sorted)


def _combine_kernel(final, n_steps, dcur_ref, dnext_ref, x_ref, gt_ref, gates_ref, ys_ref, *rest):
    if final:
        fg_ref, o_ref, buf_ref, sems = rest
    else:
        o_ref, buf_ref, sems = rest
    i = pl.program_id(0)
    tm = x_ref.shape[0]

    def row_copy(dref, t, k, slot):
        return pltpu.make_async_copy(ys_ref.at[pl.ds(dref[t * TOP_K + k], 1)], buf_ref.at[slot, k, pl.ds(t, 1)],
                                     sems.at[slot])

    def fetch(dref, slot):
        def body(t, carry):
            for k in range(TOP_K):
                row_copy(dref, t, k, slot).start()
            return carry
        lax.fori_loop(0, tm, body, 0, unroll=ROW_DMA_UNROLL)

    slot = i % 2

    @pl.when(i == 0)
    def _():
        fetch(dcur_ref, 0)

    @pl.when(i + 1 < n_steps)
    def _():
        fetch(dnext_ref, 1 - slot)

    def wait(t, carry):
        for k in range(TOP_K):
            row_copy(dcur_ref, t, k, slot).wait()
        return carry

    lax.fori_loop(0, tm, wait, 0, unroll=ROW_DMA_UNROLL)
    gates = gates_ref[...]
    y = gates[:, 0:1] * buf_ref[slot, 0]
    for k in range(1, TOP_K):
        y = y + gates[:, k:k + 1] * buf_ref[slot, k]
    xn = x_ref[...] + gt_ref[...] * y
    if final:
        xn = xn * lax.rsqrt(jnp.mean(xn * xn, axis=-1, keepdims=True) + NORM_EPS) * fg_ref[...]
    o_ref[...] = xn


def _combine(rows, x, gate2, y_slots, dest_flat, gates, final_g):
    d = rows.d
    tm = rows.tm
    n_steps = rows.grid
    final = final_g is not None
    in_specs = [pl.BlockSpec((tm * TOP_K,), lambda i: (i,), memory_space=pltpu.SMEM),
                pl.BlockSpec((tm * TOP_K,), lambda i: (jnp.minimum(i + 1, n_steps - 1),), memory_space=pltpu.SMEM),
                rows.x_spec(), rows.mod_spec(), pl.BlockSpec((tm, LANES), lambda i: (i, 0)),
                pl.BlockSpec(memory_space=pl.ANY)]
    args = [dest_flat, dest_flat, x, rows.mod(gate2), gates, y_slots]
    if final:
        in_specs.append(_const_spec((1, d)))
        args.append(final_g.reshape(1, d))
    return pl.pallas_call(
        functools.partial(_combine_kernel, final, n_steps),
        grid=(n_steps,),
        in_specs=in_specs,
        out_specs=rows.x_spec(),
        out_shape=jax.ShapeDtypeStruct((rows.n, d), F32),
        scratch_shapes=[pltpu.VMEM((2, TOP_K, tm, d), F32), pltpu.SemaphoreType.DMA((2,))],
        compiler_params=_params("arbitrary"),
        name="moe_combine",
    )(*args)


def _moe_layer(layer, rows_p, rows_s, xp, xs, g, mod_p, mod_s, w_r, b_r, w1, b1, w2, b2, final_g, tmb):
    d = rows_p.d
    n_exp = w_r.shape[1]
    zero = jnp.zeros((1, LANES), F32)
    hp, idx_p, gates_p, rank_p, cnt = _router(rows_p, xp, g, mod_p[3], mod_p[4], w_r, b_r, zero)
    hs, idx_s, gates_s, rank_s, cnt = _router(rows_s, xs, g, mod_s[3], mod_s[4], w_r, b_r, cnt)
    n = rows_p.n + rows_s.n
    counts = cnt[0, :n_exp].astype(jnp.int32)
    padded = (counts + tmb - 1) // tmb * tmb
    pad_ends = jnp.cumsum(padded)
    pad_starts = pad_ends - padded
    lane_e = jnp.arange(n_exp, dtype=jnp.int32)

    def slots(idx, rank):
        starts = jnp.sum(jnp.where(idx[:, :TOP_K, None] == lane_e, pad_starts, 0), axis=-1)
        return (starts + rank[:, :TOP_K]).reshape(-1)

    dest_p = slots(idx_p, rank_p)
    dest_s = slots(idx_s, rank_s)
    n_blocks = -(-n * TOP_K // tmb) + n_exp
    block_start = jnp.arange(n_blocks, dtype=jnp.int32) * tmb
    block_expert = jnp.minimum(jnp.sum((pad_ends[None, :] <= block_start[:, None]).astype(jnp.int32), axis=1),
                               n_exp - 1)
    n_used = (pad_ends[-1:] // tmb).astype(jnp.int32)
    x_sorted = jnp.zeros((n_blocks * tmb, d), F32)
    x_sorted = _dispatch(rows_p, hp, dest_p, x_sorted)
    x_sorted = _dispatch(rows_s, hs, dest_s, x_sorted)
    y_slots = _experts(x_sorted, block_expert, n_used, w1, b1, w2, b2, layer, tmb)
    xp = _combine(rows_p, xp, mod_p[5], y_slots, dest_p, gates_p, final_g)
    xs = _combine(rows_s, xs, mod_s[5], y_slots, dest_s, gates_s, final_g)
    return xp, xs


def _proj_res_kernel(a_ref, w_ref, x_ref, gt_ref, o_ref):
    o_ref[...] = x_ref[...] + gt_ref[...] * jnp.dot(a_ref[...], w_ref[...], preferred_element_type=F32)


def _proj_residual(rows, a, w, x, gate):
    d = rows.d
    kdim = w.shape[0]
    return pl.pallas_call(
        _proj_res_kernel,
        grid=(rows.grid,),
        in_specs=[rows.x_spec(kdim), _const_spec((kdim, d)), rows.x_spec(), rows.mod_spec()],
        out_specs=rows.x_spec(),
        out_shape=jax.ShapeDtypeStruct((rows.n, d), F32),
        compiler_params=_params("arbitrary"),
        name="proj_residual",
    )(a, w.astype(BF16), x, rows.mod(gate))


def _nt_dot(a, b, precision=None):
    return lax.dot_general(a, b, (((1,), (1,)), ((), ())), precision=precision, preferred_element_type=F32)


def _fox_proj_kernel(seq_tiles, group, x_ref, g_ref, sh_ref, sc_ref, wqkv_ref, wf_ref, bf_ref,
                     q_ref, k_ref, v_ref, lf_ref, cum_ref, carry_ref):
    i = pl.program_id(0)
    tm, d = x_ref.shape
    h = _norm_mod(x_ref[...], g_ref[...], sh_ref[...], sc_ref[...])
    qkv = _bdot(h, wqkv_ref[...])
    q_ref[...] = qkv[:, :d]
    k_ref[...] = qkv[:, d:2 * d]
    v_ref[...] = qkv[:, 2 * d:]
    z = jnp.dot(h, wf_ref[...], precision=HIGHEST, preferred_element_type=F32) + bf_ref[...]
    lf = jnp.minimum(z, 0.0) - jnp.log1p(jnp.exp(-jnp.abs(z)))
    lf_ref[...] = lf
    r = lax.broadcasted_iota(jnp.int32, (tm, tm), 0)
    c = lax.broadcasted_iota(jnp.int32, (tm, tm), 1)
    if seq_tiles:
        @pl.when(i % seq_tiles == 0)
        def _():
            carry_ref[...] = jnp.zeros_like(carry_ref)

        tri = jnp.where(c <= r, 1.0, 0.0)
        cum_ref[...] = jnp.dot(tri, lf, precision=HIGHEST, preferred_element_type=F32) + carry_ref[...]
        carry_ref[...] = carry_ref[...] + jnp.sum(lf, axis=0, keepdims=True)
    else:
        tri = jnp.where(jnp.logical_and(c <= r, r // group == c // group), 1.0, 0.0)
        cum_ref[...] = jnp.dot(tri, lf, precision=HIGHEST, preferred_element_type=F32)


def _fox_proj(rows, x, g, shift, scale, w_qkv, w_f, b_f):
    d = rows.d
    nh = w_f.shape[1]
    wf = jnp.pad(w_f, ((0, 0), (0, LANES - nh)))
    bf = jnp.pad(b_f, (0, LANES - nh)).reshape(1, LANES)
    seq_tiles = rows.seq_len // rows.tm if rows.per_seq else 0
    lane_spec = pl.BlockSpec((rows.tm, LANES), lambda i: (i, 0))
    return pl.pallas_call(
        functools.partial(_fox_proj_kernel, seq_tiles, rows.seq_len),
        grid=(rows.grid,),
        in_specs=[rows.x_spec(), _const_spec((1, d)), rows.mod_spec(), rows.mod_spec(),
                  _const_spec((d, 3 * d)), _const_spec((d, LANES)), _const_spec((1, LANES))],
        out_specs=[rows.x_spec(), rows.x_spec(), rows.x_spec(), lane_spec, lane_spec],
        out_shape=[jax.ShapeDtypeStruct((rows.n, d), F32)] * 3 + [jax.ShapeDtypeStruct((rows.n, LANES), F32)] * 2,
        scratch_shapes=[pltpu.VMEM((1, LANES), F32)],
        compiler_params=_params("arbitrary"),
        name="fox_proj",
    )(x, g.reshape(1, d), rows.mod(shift), rows.mod(scale), w_qkv.astype(BF16), wf, bf)


def _flash_kernel(qi_ref, ki_ref, q_ref, k_ref, v_ref, o_ref, m_ref, l_ref, acc_ref):
    p_id = pl.program_id(1)
    qi = qi_ref[p_id]
    ki = ki_ref[p_id]
    tq = q_ref.shape[1]
    tk = k_ref.shape[1]
    nh = q_ref.shape[2] // LANES

    @pl.when(ki == 0)
    def _():
        m_ref[...] = jnp.full(m_ref.shape, -jnp.inf, F32)
        l_ref[...] = jnp.zeros(l_ref.shape, F32)
        acc_ref[...] = jnp.zeros(acc_ref.shape, F32)

    def scores(h):
        sl = slice(h * LANES, (h + 1) * LANES)
        return _nt_dot(q_ref[0, :, sl], k_ref[0, :, sl])

    def body(masked):
        if masked:
            keep = (lax.broadcasted_iota(jnp.int32, (tq, tk), 1) <= lax.broadcasted_iota(jnp.int32, (tq, tk), 0))
        s_next = scores(0)
        for h in range(nh):
            sl = slice(h * LANES, (h + 1) * LANES)
            s = s_next
            if h + 1 < nh:
                s_next = scores(h + 1)
            if masked:
                s = jnp.where(keep, s, -jnp.inf)
            m_prev = m_ref[h]
            m_new = jnp.maximum(m_prev, jnp.max(s, axis=-1, keepdims=True))
            corr = jnp.exp(m_prev - m_new)
            p = jnp.exp(s - jnp.tile(m_new, (1, tk // LANES)))
            l_ref[h] = l_ref[h] * corr + jnp.sum(p, axis=-1, keepdims=True)
            acc_ref[:, sl] = acc_ref[:, sl] * corr + jnp.dot(p.astype(BF16), v_ref[0, :, sl],
                                                            preferred_element_type=F32)
            m_ref[h] = m_new

    @pl.when(ki < qi)
    def _():
        body(False)

    @pl.when(ki == qi)
    def _():
        body(True)
        for h in range(nh):
            sl = slice(h * LANES, (h + 1) * LANES)
            o_ref[0, :, sl] = (acc_ref[:, sl] / l_ref[h]).astype(o_ref.dtype)


def _bf16_parts(c):
    c1 = c.astype(BF16).astype(F32)
    r1 = c - c1
    c2 = r1.astype(BF16).astype(F32)
    c3 = (r1 - c2).astype(BF16).astype(F32)
    return c1, c2, c3


def _fox_head_layout(q, k, v, cum, nh):
    b, t, d = q.shape
    hd = d // nh
    parts = [p[..., None] for p in _bf16_parts(cum[:, :, :nh])]
    one = jnp.ones((b, t, nh, 1), F32)
    fill = jnp.zeros((b, t, nh, LANES - hd - 6), F32)
    qh = (q * hd ** -0.5).reshape(b, t, nh, hd)
    q_l = jnp.concatenate([qh] + parts + [one] * 3 + [fill], axis=-1)
    k_l = jnp.concatenate([k.reshape(b, t, nh, hd)] + [one] * 3 + [-p for p in parts] + [fill], axis=-1)
    v_l = jnp.pad(v.reshape(b, t, nh, hd), ((0, 0), (0, 0), (0, 0), (0, LANES - hd)))
    return tuple(z.astype(BF16).reshape(b, t, nh * LANES) for z in (q_l, k_l, v_l))


def _fox_flash(q_l, k_l, v_l, tile):
    b, t, dl = q_l.shape
    nh = dl // LANES
    tq = _tile(t, tile)
    nq = t // tq
    pairs = [(a, c) for a in range(nq) for c in range(a + 1)]
    qi = jnp.array([p[0] for p in pairs], jnp.int32)
    ki = jnp.array([p[1] for p in pairs], jnp.int32)
    q_spec = pl.BlockSpec((1, tq, dl), lambda bi, p, qi, ki: (bi, qi[p], 0))
    k_spec = pl.BlockSpec((1, tq, dl), lambda bi, p, qi, ki: (bi, ki[p], 0))
    grid_spec = pltpu.PrefetchScalarGridSpec(
        num_scalar_prefetch=2,
        grid=(b, len(pairs)),
        in_specs=[q_spec, k_spec, k_spec],
        out_specs=q_spec,
        scratch_shapes=[pltpu.VMEM((nh, tq, LANES), F32), pltpu.VMEM((nh, tq, LANES), F32),
                        pltpu.VMEM((tq, dl), F32)],
    )
    return pl.pallas_call(
        _flash_kernel,
        grid_spec=grid_spec,
        out_shape=jax.ShapeDtypeStruct((b, t, dl), BF16),
        compiler_params=_params("arbitrary", "arbitrary"),
        name="fox_flash",
    )(qi, ki, q_l, k_l, v_l)


def _paged_kernel(n_pages, nh, pt_ref, q_ref, kn_ref, vn_ref, g_ref, kc_ref, vc_ref, lfc_ref, o_ref,
                  qbd_ref, acc_ref, m_ref, l_ref, carry_ref, lfp_ref):
    s_id = pl.program_id(1)
    dseq, d = q_ref.shape
    page = kc_ref.shape[1]
    rws = nh * dseq
    hd = d // nh
    row_h = lax.broadcasted_iota(jnp.int32, (rws, LANES), 0) // dseq
    lane = lax.broadcasted_iota(jnp.int32, (rws, LANES), 1)
    head_sel = jnp.where(lane == row_h, 1.0, 0.0)
    g_new = g_ref[...]
    gq = jnp.sum(head_sel * jnp.concatenate([g_new] * nh, axis=0), axis=-1, keepdims=True)

    def online(s, v_bf):
        m_prev = m_ref[...]
        m_new = jnp.maximum(m_prev, jnp.max(s, axis=-1, keepdims=True))
        corr = jnp.exp(m_prev - m_new)
        p = jnp.exp(s - m_new)
        l_ref[...] = l_ref[...] * corr + jnp.sum(p, axis=-1, keepdims=True)
        acc_ref[...] = acc_ref[...] * corr + jnp.dot(p.astype(BF16), v_bf, preferred_element_type=F32)
        m_ref[...] = m_new

    @pl.when(s_id == 0)
    def _():
        on_head = (lax.broadcasted_iota(jnp.int32, (rws, d), 0) // dseq
                   == lax.broadcasted_iota(jnp.int32, (rws, d), 1) // hd)
        q_rows = jnp.concatenate([q_ref[...] * (hd ** -0.5)] * nh, axis=0)
        qbd_ref[...] = jnp.where(on_head, q_rows, 0.0).astype(BF16)
        m_ref[...] = jnp.full(m_ref.shape, -jnp.inf, F32)
        l_ref[...] = jnp.zeros(l_ref.shape, F32)
        acc_ref[...] = jnp.zeros(acc_ref.shape, F32)
        carry_ref[...] = jnp.zeros(carry_ref.shape, F32)
        lfp_ref[...] = jnp.zeros(lfp_ref.shape, F32)
        pad = jnp.zeros((page - dseq, d), F32)
        k_pad = jnp.concatenate([kn_ref[...], pad], axis=0).astype(BF16)
        v_pad = jnp.concatenate([vn_ref[...], pad], axis=0).astype(BF16)
        g_pad = jnp.concatenate([g_new, jnp.zeros((page - dseq, LANES), F32)], axis=0)
        gk = _nt_dot(head_sel, g_pad, HIGHEST)
        t_row = lax.broadcasted_iota(jnp.int32, (rws, page), 0) % dseq
        u_col = lax.broadcasted_iota(jnp.int32, (rws, page), 1)
        s = _nt_dot(qbd_ref[...], k_pad) + (gq - gk)
        s = jnp.where(u_col <= t_row, s, -jnp.inf)
        online(s, v_pad)

    @pl.when(s_id > 0)
    def _():
        lfp_ref[:, 0:nh] = lfc_ref[0]
        lfp = lfp_ref[...]
        pj = lax.broadcasted_iota(jnp.int32, (page, page), 0)
        jj = lax.broadcasted_iota(jnp.int32, (page, page), 1)
        later = jnp.where(jj > pj, 1.0, 0.0)
        tail_t = jnp.dot(later, lfp, precision=HIGHEST, preferred_element_type=F32) + carry_ref[...]
        bias = _nt_dot(head_sel, tail_t, HIGHEST) + gq
        carry_ref[...] = carry_ref[...] + jnp.sum(lfp, axis=0, keepdims=True)
        s = _nt_dot(qbd_ref[...], kc_ref[0].astype(BF16)) + bias
        online(s, vc_ref[0].astype(BF16))

    @pl.when(s_id == n_pages)
    def _():
        on_head = (lax.broadcasted_iota(jnp.int32, (rws, d), 0) // dseq
                   == lax.broadcasted_iota(jnp.int32, (rws, d), 1) // hd)
        o_full = jnp.where(on_head, acc_ref[...] / l_ref[...], 0.0)
        o_ref[...] = jnp.sum(o_full.reshape(nh, dseq, d), axis=0).astype(o_ref.dtype)


def _fox_paged(q, k_new, v_new, cum_new, cache_k, cache_v, cache_lf, page_table, dseq, layer):
    ns, d = q.shape
    bsz, n_pages = page_table.shape
    n_layers, pool, page, nh = cache_lf.shape
    rws = nh * dseq

    def page_idx(bi, s, pt):
        return layer * pool + pt[bi * n_pages + n_pages - jnp.maximum(s, 1)]

    seq_spec = pl.BlockSpec((dseq, d), lambda bi, s, pt: (bi, 0))
    grid_spec = pltpu.PrefetchScalarGridSpec(
        num_scalar_prefetch=1,
        grid=(bsz, n_pages + 1),
        in_specs=[
            seq_spec, seq_spec, seq_spec,
            pl.BlockSpec((dseq, LANES), lambda bi, s, pt: (bi, 0)),
            pl.BlockSpec((1, page, d), lambda bi, s, pt: (page_idx(bi, s, pt), 0, 0)),
            pl.BlockSpec((1, page, d), lambda bi, s, pt: (page_idx(bi, s, pt), 0, 0)),
            pl.BlockSpec((1, page, nh), lambda bi, s, pt: (page_idx(bi, s, pt), 0, 0)),
        ],
        out_specs=seq_spec,
        scratch_shapes=[pltpu.VMEM((rws, d), BF16), pltpu.VMEM((rws, d), F32), pltpu.VMEM((rws, 1), F32),
                        pltpu.VMEM((rws, 1), F32), pltpu.VMEM((1, LANES), F32), pltpu.VMEM((page, LANES), F32)],
    )
    return pl.pallas_call(
        functools.partial(_paged_kernel, n_pages, nh),
        grid_spec=grid_spec,
        out_shape=jax.ShapeDtypeStruct((ns, d), BF16),
        compiler_params=_params("arbitrary", "arbitrary"),
        name="fox_paged",
    )(page_table.reshape(-1), q, k_new, v_new, cum_new,
      cache_k.reshape(n_layers * pool, page, d), cache_v.reshape(n_layers * pool, page, d),
      cache_lf.reshape(n_layers * pool, page, nh))


def _fox_layer(layer, rows_p, rows_s, xp, xs, g, mod_p, mod_s, cache_k, cache_v, cache_lf, page_table,
               w_qkv, w_f, b_f, w_o, flash_tile):
    d = rows_p.d
    nh = w_f.shape[1]
    bp, t = rows_p.n_seq, rows_p.seq_len
    qp, kp, vp, lfp, cump = _fox_proj(rows_p, xp, g, mod_p[0], mod_p[1], w_qkv, w_f, b_f)
    qs, ks, vs, lfs, cums = _fox_proj(rows_s, xs, g, mod_s[0], mod_s[1], w_qkv, w_f, b_f)
    q_l, k_l, v_l = _fox_head_layout(qp.reshape(bp, t, d), kp.reshape(bp, t, d), vp.reshape(bp, t, d),
                                     cump.reshape(bp, t, LANES), nh)
    op = _fox_flash(q_l, k_l, v_l, flash_tile)
    os_ = _fox_paged(qs, ks, vs, cums, cache_k, cache_v, cache_lf, page_table, rows_s.seq_len, layer)
    hd = d // nh
    w_o_l = jnp.pad(w_o.reshape(nh, hd, d), ((0, 0), (0, LANES - hd), (0, 0))).reshape(nh * LANES, d)
    xp = _proj_residual(rows_p, op.reshape(-1, nh * LANES), w_o_l, xp, mod_p[2])
    xs = _proj_residual(rows_s, os_, w_o, xs, mod_s[2])
    return xp, xs, (kp, vp, lfp[:, :nh]), (ks, vs, lfs[:, :nh])


def _log_sigmoid(z):
    return jnp.minimum(z, 0.0) - jnp.log1p(jnp.exp(-jnp.abs(z)))


def _head_sum(x, hd):
    i = lax.broadcasted_iota(jnp.int32, (LANES, LANES), 0) // hd
    j = lax.broadcasted_iota(jnp.int32, (LANES, LANES), 1) // hd
    same = jnp.where(i == j, 1.0, 0.0)
    parts = [jnp.dot(x[:, c * LANES:(c + 1) * LANES], same, precision=HIGHEST, preferred_element_type=F32)
             for c in range(x.shape[1] // LANES)]
    return jnp.concatenate(parts, axis=1)


def _rwkv_proj_kernel(seq_tiles, group, x_ref, xprev_ref, sh_ref, g_ref, shift_ref, scale_ref, mu_ref, wrkv_ref,
                      w0_ref, w1_ref, w2_ref, a0_ref, a1_ref, a2_ref, g1_ref, g2_ref, kk_ref, ka_ref, rk_ref,
                      r_out, lw_out, k_out, v_out, a_out, b_out, bonus_out, gate_out, h_out):
    i = pl.program_id(0)
    tm, d = x_ref.shape
    gvec, shift, scale = g_ref[...], shift_ref[...], scale_ref[...]
    h = _norm_mod(x_ref[...], gvec, shift, scale)
    rolled = pltpu.roll(h, 1, 0)
    row = lax.broadcasted_iota(jnp.int32, (tm, d), 0)
    if seq_tiles:
        h_before = _norm_mod(xprev_ref[...], gvec, shift, scale)[7:8]
        first = jnp.where(i % seq_tiles == 0, sh_ref[...], h_before)
        x_prev = jnp.where(row == 0, first, rolled)
        h_out[...] = h[tm - 8:]
    else:
        x_prev = jnp.where(row % group == 0, sh_ref[...], rolled)
        h_out[...] = h
    xx = x_prev - h
    xr, xw, xk, xv, xa, xg = (h + xx * mu_ref[n:n + 1] for n in range(6))
    r = _bdot(xr, wrkv_ref[0])
    k = _bdot(xk, wrkv_ref[1])
    v = _bdot(xv, wrkv_ref[2])
    w_log = _log_sigmoid(w0_ref[...] + _bdot(jnp.tanh(_bdot(xw, w1_ref[...])), w2_ref[...])) - 0.5
    iclr = jax.nn.sigmoid(a0_ref[...] + _bdot(_bdot(xa, a1_ref[...]), a2_ref[...]))
    gate_out[...] = _bdot(jax.nn.sigmoid(_bdot(xg, g1_ref[...])), g2_ref[...])
    kk = k * kk_ref[...]
    kk = kk / jnp.maximum(jnp.sqrt(_head_sum(kk * kk, RW_HEAD)), 1e-12)
    k2 = k * (1.0 + (iclr - 1.0) * ka_ref[...])
    r_out[...] = r
    lw_out[...] = -jnp.exp(w_log)
    k_out[...] = k2
    v_out[...] = v
    a_out[...] = -kk
    b_out[...] = kk * iclr
    bonus_out[...] = _head_sum(r * k2 * rk_ref[...], RW_HEAD) * v


def _pad_to(x, axis, size):
    pad = [(0, 0)] * x.ndim
    pad[axis] = (0, size - x.shape[axis])
    return jnp.pad(x, pad)


def _rwkv_proj(rows, x, g, shift, scale, shift_state, mu, w_rkv, w0, w1, w2, a0, a1, a2, g1, g2, k_k, k_a, r_k):
    d = rows.d
    lw_, la_, lg_ = (-(-w.shape[1] // LANES) * LANES for w in (w1, a1, g1))
    seq_tiles = rows.seq_len // rows.tm if rows.per_seq else 0
    if rows.per_seq:
        per8 = rows.tm // 8
        xprev_spec = pl.BlockSpec((8, d), lambda i: (jnp.maximum(i * per8 - 1, 0), 0))
        h_shape = jax.ShapeDtypeStruct((rows.grid * 8, d), F32)
        h_spec = pl.BlockSpec((8, d), lambda i: (i, 0))
    else:
        xprev_spec = pl.BlockSpec((8, d), lambda i: (0, 0))
        h_shape = jax.ShapeDtypeStruct((rows.n, d), F32)
        h_spec = rows.x_spec()
    vec = _const_spec((1, d))
    out = pl.pallas_call(
        functools.partial(_rwkv_proj_kernel, seq_tiles, rows.seq_len),
        grid=(rows.grid,),
        in_specs=[rows.x_spec(), xprev_spec, rows.mod_spec(), vec, rows.mod_spec(), rows.mod_spec(),
                  _const_spec((6, d)), _const_spec((3, d, d)),
                  vec, _const_spec((d, lw_)), _const_spec((lw_, d)),
                  vec, _const_spec((d, la_)), _const_spec((la_, d)),
                  _const_spec((d, lg_)), _const_spec((lg_, d)), vec, vec, vec],
        out_specs=[rows.x_spec()] * 8 + [h_spec],
        out_shape=[jax.ShapeDtypeStruct((rows.n, d), F32)] * 8 + [h_shape],
        compiler_params=_params("arbitrary"),
        name="rwkv_proj",
    )(x, x, rows.mod(shift_state), g.reshape(1, d), rows.mod(shift), rows.mod(scale), mu, w_rkv.astype(BF16),
      w0.reshape(1, d), _pad_to(w1, 1, lw_).astype(BF16), _pad_to(w2, 0, lw_).astype(BF16),
      a0.reshape(1, d), _pad_to(a1, 1, la_).astype(BF16), _pad_to(a2, 0, la_).astype(BF16),
      _pad_to(g1, 1, lg_).astype(BF16), _pad_to(g2, 0, lg_).astype(BF16),
      k_k.reshape(1, d), k_a.reshape(1, d), r_k.reshape(1, d))
    return out


def _tn_dot(a, b):
    return lax.dot_general(a.astype(BF16), b.astype(BF16), (((0,), (0,)), ((), ())), preferred_element_type=F32)


def _wkv_kernel(chunk, r_ref, lw_ref, k_ref, v_ref, a_ref, b_ref, s0_ref, y_ref, sT_ref, st_ref):
    tt_id = pl.program_id(2)
    nseq, tt, width = r_ref.shape
    hd = RW_HEAD
    nhead = width // hd
    ln = chunk
    nchunk = tt // ln

    @pl.when(tt_id == 0)
    def _():
        st_ref[...] = s0_ref[...]

    rr = lax.broadcasted_iota(jnp.int32, (ln, ln), 0)
    cc = lax.broadcasted_iota(jnp.int32, (ln, ln), 1)
    tri = jnp.where(cc <= rr, 1.0, 0.0)
    eye = jnp.where(cc == rr, 1.0, 0.0)
    r2 = lax.broadcasted_iota(jnp.int32, (2 * ln, 2 * ln), 0)
    c2 = lax.broadcasted_iota(jnp.int32, (2 * ln, 2 * ln), 1) % ln
    keep = c2 < jnp.where(r2 < ln, r2, r2 - ln + 1)
    n_double = ln.bit_length() - 2
    items = [(s, c, j) for s in range(nseq) for c in range(nchunk) for j in range(nhead)]

    scaled = {}
    for s in range(nseq):
        for c in range(nchunk):
            rows = slice(c * ln, (c + 1) * ln)
            lw = lw_ref[s, rows, :]
            cw = jnp.dot(tri, lw, precision=HIGHEST, preferred_element_type=F32)
            e_neg = jnp.exp(-cw)
            e_all = jnp.exp(cw[ln - 1:ln, :])
            b_t = b_ref[s, rows, :] * e_neg
            k_t = k_ref[s, rows, :] * e_neg
            scaled[s, c] = (a_ref[s, rows, :] * jnp.exp(cw - lw), r_ref[s, rows, :] * jnp.exp(cw), b_t, k_t,
                            b_t * e_all, k_t * e_all, e_all, v_ref[s, rows, :])

    gram, a_t, r_t, vj, ends, e_all = {}, {}, {}, {}, {}, {}
    for it in items:
        s, c, j = it
        sl = slice(j * hd, (j + 1) * hd)
        at, rt, bt, kt, b_end, k_end, ea, vv = scaled[s, c]
        a_t[it], r_t[it], vj[it], e_all[it] = at[:, sl], rt[:, sl], vv[:, sl], ea[:, sl]
        ends[it] = jnp.concatenate([b_end[:, sl], k_end[:, sl]], axis=0)
        lhs = jnp.concatenate([at[:, sl], rt[:, sl]], axis=0)
        rhs = jnp.concatenate([bt[:, sl], kt[:, sl]], axis=0)
        gram[it] = jnp.where(keep, _nt_dot(lhs.astype(BF16), rhs.astype(BF16)), 0.0)
    inv = {it: eye + gram[it][:ln, :ln] for it in items}
    power = {it: gram[it][:ln, :ln] for it in items}
    for _ in range(n_double):
        power = {it: _bdot(power[it], power[it]) for it in items}
        inv = {it: inv[it] + _bdot(inv[it], power[it]) for it in items}
    x0 = {it: _bdot(gram[it][:ln, ln:], vj[it]) for it in items}
    gu = {it: _bdot(inv[it], jnp.concatenate([a_t[it], x0[it]], axis=1)) for it in items}
    uv = {it: jnp.concatenate([gu[it][:, hd:], vj[it]], axis=0) for it in items}
    q_r = {it: r_t[it] + _bdot(gram[it][ln:, :ln], gu[it][:, :hd]) for it in items}
    y0 = {it: _bdot(gram[it][ln:, :], uv[it]) for it in items}
    m_st = {it: _tn_dot(gu[it][:, :hd], ends[it][:ln]) for it in items}
    c_st = {it: _tn_dot(uv[it], ends[it]) for it in items}

    for s in range(nseq):
        states = [st_ref[s, j] for j in range(nhead)]
        for c in range(nchunk):
            ys = []
            for j in range(nhead):
                it = (s, c, j)
                ys.append(_nt_dot(q_r[it].astype(BF16), states[j].astype(BF16)) + y0[it])
                states[j] = states[j] * e_all[it] + _bdot(states[j], m_st[it]) + c_st[it]
            y_ref[s, c * ln:(c + 1) * ln, :] = jnp.concatenate(ys, axis=1)
        for j in range(nhead):
            st_ref[s, j] = states[j]

    @pl.when(tt_id == pl.num_programs(2) - 1)
    def _():
        sT_ref[...] = st_ref[...]


def _wkv(r, lw, k, v, a, b, state0, chunk, tile, nseq, width):
    bsz, t, d = r.shape
    tt = _tile(t, tile)
    assert tt % chunk == 0 and bsz % nseq == 0 and d % width == 0
    nhead = width // RW_HEAD
    seq_spec = pl.BlockSpec((nseq, tt, width), lambda bi, hp, ti: (bi, ti, hp))
    st_spec = pl.BlockSpec((nseq, nhead, RW_HEAD, RW_HEAD), lambda bi, hp, ti: (bi, hp, 0, 0))
    return pl.pallas_call(
        functools.partial(_wkv_kernel, chunk),
        grid=(bsz // nseq, d // width, t // tt),
        in_specs=[seq_spec] * 6 + [st_spec],
        out_specs=[seq_spec, st_spec],
        out_shape=[jax.ShapeDtypeStruct((bsz, t, d), F32), jax.ShapeDtypeStruct(state0.shape, F32)],
        scratch_shapes=[pltpu.VMEM((nseq, nhead, RW_HEAD, RW_HEAD), F32)],
        compiler_params=_params("arbitrary", "arbitrary", "arbitrary"),
        name="wkv",
    )(r, lw, k, v, a, b, state0)


def _rwkv_out_kernel(y_ref, bonus_ref, gate_ref, x_ref, gt_ref, lng_ref, lnb_ref, wo_ref, o_ref):
    y = y_ref[...]
    mean = _head_sum(y, RW_HEAD) * (1.0 / RW_HEAD)
    yc = y - mean
    var = _head_sum(yc * yc, RW_HEAD) * (1.0 / RW_HEAD)
    yn = yc * lax.rsqrt(var + RW_GN_EPS) * lng_ref[...] + lnb_ref[...] + bonus_ref[...]
    o_ref[...] = x_ref[...] + gt_ref[...] * _bdot(yn * gate_ref[...], wo_ref[...])


def _rwkv_out(rows, y, bonus, gate, x, gate1, ln_g, ln_b, w_o):
    d = rows.d
    return pl.pallas_call(
        _rwkv_out_kernel,
        grid=(rows.grid,),
        in_specs=[rows.x_spec()] * 4 + [rows.mod_spec(), _const_spec((1, d)), _const_spec((1, d)), _const_spec((d, d))],
        out_specs=rows.x_spec(),
        out_shape=jax.ShapeDtypeStruct((rows.n, d), F32),
        compiler_params=_params("arbitrary"),
        name="rwkv_out",
    )(y, bonus, gate, x, rows.mod(gate1), ln_g.reshape(1, d), ln_b.reshape(1, d), w_o.astype(BF16))


def _rwkv_group(rows, x, g, mod, shift_state, wkv_state, mu, w_rkv, w_o, w0, w1, w2, a0, a1, a2, g1, g2,
                k_k, k_a, r_k, ln_g, ln_b, chunk, tile, nseq):
    d = rows.d
    bsz, t = rows.n_seq, rows.seq_len
    r, lw, k, v, a, b, bonus, gate, h_tail = _rwkv_proj(rows, x, g, mod[0], mod[1], shift_state, mu, w_rkv,
                                                        w0, w1, w2, a0, a1, a2, g1, g2, k_k, k_a, r_k)
    to3 = lambda z: z.reshape(bsz, t, d)
    y, state = _wkv(to3(r), to3(lw), to3(k), to3(v), to3(a), to3(b), wkv_state, chunk, tile, nseq, WKV_LANES)
    x_new = _rwkv_out(rows, y.reshape(-1, d), bonus, gate, x, mod[2], ln_g, ln_b, w_o)
    if rows.per_seq:
        h_last = h_tail.reshape(bsz, -1, d)[:, -1]
    else:
        h_last = h_tail.reshape(bsz, t, d)[:, -1]
    return x_new, h_last, state


def kernel(x_prompt, x_sample, cache_fox_k, cache_fox_v, cache_fox_logf, state_rwkv_shift, state_rwkv_wkv, page_table, c_prompt, c_sample, norm1_g, norm2_g, ada_w, ada_b, cm_w_in, cm_b_in, cm_ln_g, cm_ln_b, cm_w_s, cm_b_s, cm_w_out, cm_b_out, rw_mu, rw_w_rkv, rw_w_o, rw_w0, rw_w1, rw_w2, rw_a0, rw_a1, rw_a2, rw_g1, rw_g2, rw_k_k, rw_k_a, rw_r_k, rw_ln_g, rw_ln_b, fox_w_qkv, fox_w_f, fox_b_f, fox_w_o, moe_wr, moe_br, moe_w1, moe_b1, moe_w2, moe_b2, final_g):
    bp, seq, d = x_prompt.shape
    bs, dseq, _ = x_sample.shape
    c_all = jnp.concatenate([c_prompt, c_sample], 0)
    n = c_all.shape[0]
    npad = -(-n // 8) * 8
    mod = _adaln(jnp.pad(c_all, ((0, npad - n), (0, 0))), ada_w, ada_b)
    depth = ada_w.shape[0]
    rows_p = _Rows(bp, seq, d, ROW_TILE)
    rows_s = _Rows(bs, dseq, d, ROW_TILE)
    width = cm_w_out.shape[1]
    nh_rw = d // RW_HEAD
    nh_fox = fox_w_f.shape[2]
    xp = x_prompt.reshape(-1, d)
    xs = x_sample.reshape(-1, d)
    cm_v_s = []
    rw_sh_p, rw_wkv_p, rw_sh_s, rw_wkv_s = [], [], [], []
    fx_p, fx_s = [], []
    for i in range(depth):
        kind, j = i % N_MIXERS, i // N_MIXERS
        mp = jnp.split(mod[i, :bp], 6, axis=-1)
        ms = jnp.split(mod[i, bp:n], 6, axis=-1)
        if kind == 0:
            cm = (cm_w_in[j], cm_b_in[j], cm_ln_g[j], cm_ln_b[j])
            ws_p, bs_p = _cm_spatial_weights(cm_w_s[j], cm_b_s[j], CHUNK, width)
            ws_s, bs_s = _cm_spatial_weights(cm_w_s[j], cm_b_s[j], dseq, width)
            (xp,) = _cm_layer(rows_p, xp, norm1_g[i], mp[0], mp[1], mp[2], *cm, ws_p, bs_p, cm_w_out[j], cm_b_out[j], False)
            xs, v_new = _cm_layer(rows_s, xs, norm1_g[i], ms[0], ms[1], ms[2], *cm, ws_s, bs_s, cm_w_out[j], cm_b_out[j], True)
            cm_v_s.append(v_new.reshape(bs, dseq, width))
        elif kind == 1:
            rw = (rw_mu[j], rw_w_rkv[j], rw_w_o[j], rw_w0[j], rw_w1[j], rw_w2[j], rw_a0[j], rw_a1[j], rw_a2[j],
                  rw_g1[j], rw_g2[j], rw_k_k[j], rw_k_a[j], rw_r_k[j], rw_ln_g[j], rw_ln_b[j])
            xp, sh_p, wkv_p = _rwkv_group(rows_p, xp, norm1_g[i], mp, jnp.zeros((bp, d), F32),
                                          jnp.zeros((bp, nh_rw, RW_HEAD, RW_HEAD), F32), *rw,
                                          min(WKV_CHUNK, seq), WKV_TILE, 1)
            xs, sh_s, wkv_s = _rwkv_group(rows_s, xs, norm1_g[i], ms, state_rwkv_shift[j], state_rwkv_wkv[j], *rw,
                                          dseq, dseq, min(WKV_DECODE_SEQS, bs))
            rw_sh_p.append(sh_p)
            rw_wkv_p.append(wkv_p)
            rw_sh_s.append(sh_s)
            rw_wkv_s.append(wkv_s)
        else:
            xp, xs, (k_p, v_p, lf_p), (k_s, v_s, lf_s) = _fox_layer(
                j, rows_p, rows_s, xp, xs, norm1_g[i], mp, ms, cache_fox_k, cache_fox_v, cache_fox_logf,
                page_table, fox_w_qkv[j], fox_w_f[j], fox_b_f[j], fox_w_o[j], FLASH_TILE)
            hd = d // nh_fox
            fx_p.append((k_p.reshape(bp, seq, nh_fox, hd), v_p.reshape(bp, seq, nh_fox, hd),
                         lf_p.reshape(bp, seq, nh_fox)))
            fx_s.append((k_s.reshape(bs, dseq, nh_fox, hd), v_s.reshape(bs, dseq, nh_fox, hd),
                         lf_s.reshape(bs, dseq, nh_fox)))
        xp, xs = _moe_layer(i, rows_p, rows_s, xp, xs, norm2_g[i], mp, ms, moe_wr[i], moe_br[i],
                            moe_w1, moe_b1, moe_w2, moe_b2, final_g if i == depth - 1 else None, MOE_ROWS)
    return (xp.reshape(bp, seq, d), xs.reshape(bs, dseq, d), jnp.stack(cm_v_s),
            jnp.stack(rw_sh_p), jnp.stack(rw_wkv_p), jnp.stack(rw_sh_s), jnp.stack(rw_wkv_s),
            jnp.stack([f[0] for f in fx_p]), jnp.stack([f[1] for f in fx_p]), jnp.stack([f[2] for f in fx_p]),
            jnp.stack([f[0] for f in fx_s]), jnp.stack([f[1] for f in fx_s]), jnp.stack([f[2] for f in fx_s]))
```

```python
import functools

import jax
import jax.numpy as jnp
from jax import lax
from jax.experimental import pallas as pl
from jax.experimental.pallas import tpu as pltpu

F32 = jnp.float32
BF16 = jnp.bfloat16
HIGHEST = lax.Precision.HIGHEST

NORM_EPS = 1e-6
N_MIXERS = 3
CHUNK = 128
CM_GROUPS = 8
RW_HEAD = 64
RW_GN_EPS = 64e-5
FOX_HEAD = 64
PAGE_SIZE = 128
TOP_K = 4
SWIGLU_LIMIT = 7.0
SWIGLU_ALPHA = 1.702
MOE_BLOCK = 128

V7X_VMEM_BYTES = 64 * 1024 * 1024
VMEM_LIMIT = V7X_VMEM_BYTES - 8 * 1024 * 1024
LANES = 128

ROW_TILE = 256
FLASH_TILE = 512
WKV_CHUNK = 64
WKV_TILE = 256
WKV_LANES = 256
WKV_DECODE_SEQS = 8
MOE_ROWS = 512
PAGES_PER_STEP = 4


def _params(*sem):
    return pltpu.CompilerParams(dimension_semantics=sem, vmem_limit_bytes=VMEM_LIMIT)


def _bdot(a, b):
    return jnp.dot(a.astype(BF16), b.astype(BF16), preferred_element_type=F32)


def _norm_mod(x, g, shift, scale):
    y = x * lax.rsqrt(jnp.mean(x * x, axis=-1, keepdims=True) + NORM_EPS)
    return (y * g) * (1.0 + scale) + shift


def _tile(n, want):
    t = min(n, want)
    assert n % t == 0, (n, want)
    return t


class _Rows:
    def __init__(self, n_seq, seq_len, d, want_tile):
        self.n_seq, self.seq_len, self.d = n_seq, seq_len, d
        self.n = n_seq * seq_len
        self.per_seq = seq_len >= want_tile or seq_len >= 128
        if self.per_seq:
            self.tm = _tile(seq_len, want_tile)
        else:
            self.tm = _tile(self.n, want_tile)
            assert self.tm % seq_len == 0
        self.grid = self.n // self.tm

    def x_spec(self, width=None):
        return pl.BlockSpec((self.tm, width or self.d), lambda i: (i, 0))

    def mod_spec(self):
        if self.per_seq:
            per = self.seq_len // self.tm
            return pl.BlockSpec((None, 1, self.d), lambda i: (i // per, 0, 0))
        return pl.BlockSpec((self.tm, self.d), lambda i: (i, 0))

    def mod(self, m):
        if self.per_seq:
            return m.reshape(self.n_seq, 1, self.d)
        return jnp.repeat(m, self.seq_len, axis=0)


def _const_spec(shape):
    nd = len(shape)
    return pl.BlockSpec(shape, lambda *_: (0,) * nd)


def _adaln_kernel(c_ref, w_ref, b_ref, o_ref):
    c = c_ref[...]
    s = c * jax.nn.sigmoid(c)
    o_ref[0] = _bdot(s, w_ref[0]) + b_ref[0]


def _adaln(c_all, ada_w, ada_b):
    depth, d, d6 = ada_w.shape
    n = c_all.shape[0]
    tn = _tile(d6, 1536)
    return pl.pallas_call(
        _adaln_kernel,
        grid=(depth, d6 // tn),
        in_specs=[
            pl.BlockSpec((n, d), lambda i, j: (0, 0)),
            pl.BlockSpec((1, d, tn), lambda i, j: (i, 0, j)),
            pl.BlockSpec((1, 1, tn), lambda i, j: (i, 0, j)),
        ],
        out_specs=pl.BlockSpec((1, n, tn), lambda i, j: (i, 0, j)),
        out_shape=jax.ShapeDtypeStruct((depth, n, d6), F32),
        compiler_params=_params("arbitrary", "arbitrary"),
        name="adaln",
    )(c_all, ada_w, ada_b.reshape(depth, 1, d6))


def _cm_kernel(emit_v, x_ref, g_ref, sh_ref, sc_ref, gt_ref, win_ref, bin_ref, lng_ref, lnb_ref, ws_ref, bs_ref,
               wout_ref, bout_ref, xo_ref, *rest):
    if emit_v:
        v_ref, gated_ref = rest
    else:
        (gated_ref,) = rest
    tm = x_ref.shape[0]
    width = lng_ref.shape[-1]
    gd = width // CM_GROUPS
    x = x_ref[...]
    h = _norm_mod(x, g_ref[...], sh_ref[...], sc_ref[...])
    z = jax.nn.gelu(_bdot(h, win_ref[...]) + bin_ref[...])
    u = z[:, :width]
    v = z[:, width:]
    mu = jnp.mean(v, axis=-1, keepdims=True)
    vc = v - mu
    var = jnp.mean(vc * vc, axis=-1, keepdims=True)
    vn = vc * lax.rsqrt(var + NORM_EPS) * lng_ref[...] + lnb_ref[...]
    if emit_v:
        v_ref[...] = vn
    vb = vn.astype(BF16)
    row = lax.broadcasted_iota(jnp.int32, (CHUNK, CHUNK), 0)
    col = lax.broadcasted_iota(jnp.int32, (CHUNK, CHUNK), 1)
    causal = col <= row
    for g in range(CM_GROUPS):
        wg = jnp.where(causal, ws_ref[g], 0.0).astype(BF16)
        for c in range(tm // CHUNK):
            r0 = c * CHUNK
            s = jnp.dot(wg, vb[r0:r0 + CHUNK, g * gd:(g + 1) * gd], preferred_element_type=F32)
            s = s + bs_ref[:, g * gd:(g + 1) * gd]
            gated_ref[r0:r0 + CHUNK, g * gd:(g + 1) * gd] = (u[r0:r0 + CHUNK, g * gd:(g + 1) * gd] * s).astype(BF16)
    y = jnp.dot(gated_ref[...], wout_ref[...], preferred_element_type=F32) + bout_ref[...]
    xo_ref[...] = x + gt_ref[...] * y


def _cm_layer(rows, x, g, shift, scale, gate, w_in, b_in, ln_g, ln_b, ws_bd, bs_rows, w_out, b_out, emit_v):
    d = rows.d
    width = w_out.shape[0]
    tm = rows.tm
    assert tm % CHUNK == 0
    out_shape = [jax.ShapeDtypeStruct((rows.n, d), F32)]
    out_specs = [rows.x_spec()]
    if emit_v:
        out_shape.append(jax.ShapeDtypeStruct((rows.n, width), F32))
        out_specs.append(rows.x_spec(width))
    res = pl.pallas_call(
        functools.partial(_cm_kernel, emit_v),
        grid=(rows.grid,),
        in_specs=[
            rows.x_spec(), _const_spec((1, d)), rows.mod_spec(), rows.mod_spec(), rows.mod_spec(),
            _const_spec((d, 2 * width)), _const_spec((1, 2 * width)), _const_spec((1, width)), _const_spec((1, width)),
            _const_spec((CM_GROUPS, CHUNK, CHUNK)), _const_spec((CHUNK, width)),
            _const_spec((width, d)), _const_spec((1, d)),
        ],
        out_specs=out_specs,
        out_shape=out_shape,
        scratch_shapes=[pltpu.VMEM((tm, width), BF16)],
        compiler_params=_params("arbitrary"),
        name="cm_layer_v" if emit_v else "cm_layer",
    )(x, g.reshape(1, d), rows.mod(shift), rows.mod(scale), rows.mod(gate),
      w_in.astype(BF16), b_in.reshape(1, -1), ln_g.reshape(1, -1), ln_b.reshape(1, -1),
      ws_bd, bs_rows, w_out.astype(BF16), b_out.reshape(1, d))
    return res


def _cm_spatial_weights(w_s, b_s, chunk_len, width):
    groups = w_s.shape[0]
    rep = CHUNK // chunk_len
    ws = w_s[:, :chunk_len, :chunk_len]
    if rep > 1:
        eye = jnp.eye(rep, dtype=w_s.dtype)
        ws = jnp.einsum("ab,grq->garbq", eye, ws).reshape(groups, CHUNK, CHUNK)
    bs = jnp.tile(b_s[:, :chunk_len].T, (rep, 1))
    return ws, jnp.repeat(bs, width // groups, axis=1)


NEG_BIG = -1e30


def _router_kernel(x_ref, g_ref, sh_ref, sc_ref, wr_ref, br_ref, cin_ref,
                   h_ref, idx_ref, gate_ref, rank_ref, cnt_ref):
    i = pl.program_id(0)
    tm = x_ref.shape[0]

    @pl.when(i == 0)
    def _():
        cnt_ref[...] = cin_ref[...]

    h = _norm_mod(x_ref[...], g_ref[...], sh_ref[...], sc_ref[...])
    h_ref[...] = h
    logits = jnp.dot(h, wr_ref[...], precision=HIGHEST, preferred_element_type=F32) + br_ref[...]
    col = lax.broadcasted_iota(jnp.int32, (tm, LANES), 1).astype(F32)
    work = logits
    hits = jnp.zeros((tm, LANES), F32)
    vals, sels = [], []
    for _ in range(TOP_K):
        m = jnp.max(work, axis=-1, keepdims=True)
        sel = jnp.min(jnp.where(work == m, col, float(LANES)), axis=-1, keepdims=True)
        hit = col == sel
        hits = hits + hit.astype(F32)
        work = jnp.where(hit, -jnp.inf, work)
        vals.append(m)
        sels.append(sel)
    es = [jnp.exp(v - vals[0]) for v in vals]
    tot = es[0] + es[1] + es[2] + es[3]
    r = lax.broadcasted_iota(jnp.int32, (tm, tm), 0)
    c = lax.broadcasted_iota(jnp.int32, (tm, tm), 1)
    below = jnp.where(c < r, 1.0, 0.0).astype(BF16)
    prefix = jnp.dot(below, hits.astype(BF16), preferred_element_type=F32) + cnt_ref[...]
    idx_o = jnp.zeros((tm, LANES), F32)
    gate_o = jnp.zeros((tm, LANES), F32)
    rank_o = jnp.zeros((tm, LANES), F32)
    for k in range(TOP_K):
        rank_k = jnp.sum(jnp.where(col == sels[k], prefix, 0.0), axis=-1, keepdims=True)
        idx_o = jnp.where(col == float(k), sels[k], idx_o)
        gate_o = jnp.where(col == float(k), es[k] / tot, gate_o)
        rank_o = jnp.where(col == float(k), rank_k, rank_o)
    idx_ref[...] = idx_o.astype(jnp.int32)
    gate_ref[...] = gate_o
    rank_ref[...] = rank_o.astype(jnp.int32)
    cnt_ref[...] = cnt_ref[...] + jnp.sum(hits, axis=0, keepdims=True)


def _router(rows, x, g, shift, scale, w_r, b_r, counts_in):
    d = rows.d
    n_exp = w_r.shape[1]
    wr = jnp.pad(w_r, ((0, 0), (0, LANES - n_exp)))
    br = jnp.pad(b_r, (0, LANES - n_exp), constant_values=NEG_BIG).reshape(1, LANES)
    lane_spec = pl.BlockSpec((rows.tm, LANES), lambda i: (i, 0))
    return pl.pallas_call(
        _router_kernel,
        grid=(rows.grid,),
        in_specs=[rows.x_spec(), _const_spec((1, d)), rows.mod_spec(), rows.mod_spec(),
                  _const_spec((d, LANES)), _const_spec((1, LANES)), _const_spec((1, LANES))],
        out_specs=[rows.x_spec(), lane_spec, lane_spec, lane_spec, _const_spec((1, LANES))],
        out_shape=[jax.ShapeDtypeStruct((rows.n, d), F32),
                   jax.ShapeDtypeStruct((rows.n, LANES), jnp.int32),
                   jax.ShapeDtypeStruct((rows.n, LANES), F32),
                   jax.ShapeDtypeStruct((rows.n, LANES), jnp.int32),
                   jax.ShapeDtypeStruct((1, LANES), F32)],
        compiler_params=_params("arbitrary"),
        name="moe_router",
    )(x, g.reshape(1, d), rows.mod(shift), rows.mod(scale), wr, br, counts_in)


def _experts_kernel(be_ref, nu_ref, x_ref, w1_ref, b1_ref, w2_ref, b2_ref, y_ref, w1b_ref, w2b_ref):
    i = pl.program_id(0)
    f = w2_ref.shape[2]
    new_expert = jnp.logical_or(i == 0, be_ref[i] != be_ref[jnp.maximum(i - 1, 0)])

    @pl.when(jnp.logical_and(new_expert, i < nu_ref[0]))
    def _():
        w1b_ref[...] = w1_ref[0, 0].astype(BF16)
        w2b_ref[...] = w2_ref[0, 0].astype(BF16)

    @pl.when(i < nu_ref[0])
    def _():
        hcat = jnp.dot(x_ref[...].astype(BF16), w1b_ref[...], preferred_element_type=F32) + b1_ref[0, 0]
        gate = jnp.minimum(hcat[:, :f], SWIGLU_LIMIT)
        up = jnp.clip(hcat[:, f:], -SWIGLU_LIMIT, SWIGLU_LIMIT)
        act = (up + 1.0) * (gate * jax.nn.sigmoid(SWIGLU_ALPHA * gate))
        y_ref[...] = jnp.dot(act.astype(BF16), w2b_ref[...], preferred_element_type=F32) + b2_ref[0, 0]

    @pl.when(i >= nu_ref[0])
    def _():
        y_ref[...] = jnp.zeros(y_ref.shape, F32)


def _experts(x_sorted, block_expert, n_used, w1, b1, w2, b2, layer, tmb):
    n_slots, d = x_sorted.shape
    _, n_exp, _, f2 = w1.shape
    f = f2 // 2
    n_blocks = n_slots // tmb
    grid_spec = pltpu.PrefetchScalarGridSpec(
        num_scalar_prefetch=2,
        grid=(n_blocks,),
        in_specs=[
            pl.BlockSpec((tmb, d), lambda i, be, nu: (i, 0)),
            pl.BlockSpec((1, 1, d, f2), lambda i, be, nu: (layer, be[i], 0, 0)),
            pl.BlockSpec((1, 1, 1, f2), lambda i, be, nu: (layer, be[i], 0, 0)),
            pl.BlockSpec((1, 1, f, d), lambda i, be, nu: (layer, be[i], 0, 0)),
            pl.BlockSpec((1, 1, 1, d), lambda i, be, nu: (layer, be[i], 0, 0)),
        ],
        out_specs=pl.BlockSpec((tmb, d), lambda i, be, nu: (i, 0)),
        scratch_shapes=[pltpu.VMEM((d, f2), BF16), pltpu.VMEM((f, d), BF16)],
    )
    return pl.pallas_call(
        _experts_kernel,
        grid_spec=grid_spec,
        out_shape=jax.ShapeDtypeStruct((n_slots, d), F32),
        compiler_params=_params("arbitrary"),
        name="moe_experts",
    )(block_expert, n_used, x_sorted, w1, b1.reshape(b1.shape[0], n_exp, 1, f2), w2, b2.reshape(b2.shape[0], n_exp, 1, d))


ROW_DMA_UNROLL = 8


def _dispatch_kernel(dest_ref, h_ref, xs_in_ref, xs_ref, sem):
    del xs_in_ref
    tm = h_ref.shape[0]

    def row_copy(t, slot):
        return pltpu.make_async_copy(h_ref.at[pl.ds(t, 1)], xs_ref.at[pl.ds(slot, 1)], sem)

    def start(t, carry):
        for k in range(TOP_K):
            row_copy(t, dest_ref[t * TOP_K + k]).start(priority=k % 2)
        return carry

    def wait(t, carry):
        for k in range(TOP_K):
            row_copy(t, dest_ref[t * TOP_K + k]).wait()
        return carry

    lax.fori_loop(0, tm, start, 0, unroll=ROW_DMA_UNROLL)
    lax.fori_loop(0, tm, wait, 0, unroll=ROW_DMA_UNROLL)


def _dispatch(rows, h, dest_flat, x_sorted):
    d = rows.d
    tm = rows.tm
    return pl.pallas_call(
        _dispatch_kernel,
        grid=(rows.grid,),
        in_specs=[pl.BlockSpec((tm * TOP_K,), lambda i: (i,), memory_space=pltpu.SMEM),
                  rows.x_spec(), pl.BlockSpec(memory_space=pl.ANY)],
        out_specs=pl.BlockSpec(memory_space=pl.ANY),
        out_shape=jax.ShapeDtypeStruct(x_sorted.shape, x_sorted.dtype),
        scratch_shapes=[pltpu.SemaphoreType.DMA(())],
        input_output_aliases={2: 0},
        compiler_params=_params("arbitrary"),
        name="moe_dispatch",
    )(dest_flat, h, x_sorted)


def _combine_kernel(final, n_steps, dcur_ref, dnext_ref, x_ref, gt_ref, gates_ref, ys_ref, *rest):
    if final:
        fg_ref, o_ref, buf_ref, sems = rest
    else:
        o_ref, buf_ref, sems = rest
    i = pl.program_id(0)
    tm = x_ref.shape[0]

    def row_copy(dref, t, k, slot):
        return pltpu.make_async_copy(ys_ref.at[pl.ds(dref[t * TOP_K + k], 1)], buf_ref.at[slot, k, pl.ds(t, 1)],
                                     sems.at[slot])

    def fetch(dref, slot):
        def body(t, carry):
            for k in range(TOP_K):
                row_copy(dref, t, k, slot).start(priority=k % 2)
            return carry
        lax.fori_loop(0, tm, body, 0, unroll=ROW_DMA_UNROLL)

    slot = i % 2

    @pl.when(i == 0)
    def _():
        fetch(dcur_ref, 0)

    @pl.when(i + 1 < n_steps)
    def _():
        fetch(dnext_ref, 1 - slot)

    def wait(t, carry):
        for k in range(TOP_K):
            row_copy(dcur_ref, t, k, slot).wait()
        return carry

    lax.fori_loop(0, tm, wait, 0, unroll=ROW_DMA_UNROLL)
    gates = gates_ref[...]
    y = gates[:, 0:1] * buf_ref[slot, 0]
    for k in range(1, TOP_K):
        y = y + gates[:, k:k + 1] * buf_ref[slot, k]
    xn = x_ref[...] + gt_ref[...] * y
    if final:
        xn = xn * lax.rsqrt(jnp.mean(xn * xn, axis=-1, keepdims=True) + NORM_EPS) * fg_ref[...]
    o_ref[...] = xn


def _combine(rows, x, gate2, y_slots, dest_flat, gates, final_g):
    d = rows.d
    tm = rows.tm
    n_steps = rows.grid
    final = final_g is not None
    in_specs = [pl.BlockSpec((tm * TOP_K,), lambda i: (i,), memory_space=pltpu.SMEM),
                pl.BlockSpec((tm * TOP_K,), lambda i: (jnp.minimum(i + 1, n_steps - 1),), memory_space=pltpu.SMEM),
                rows.x_spec(), rows.mod_spec(), pl.BlockSpec((tm, LANES), lambda i: (i, 0)),
                pl.BlockSpec(memory_space=pl.ANY)]
    args = [dest_flat, dest_flat, x, rows.mod(gate2), gates, y_slots]
    if final:
        in_specs.append(_const_spec((1, d)))
        args.append(final_g.reshape(1, d))
    return pl.pallas_call(
        functools.partial(_combine_kernel, final, n_steps),
        grid=(n_steps,),
        in_specs=in_specs,
        out_specs=rows.x_spec(),
        out_shape=jax.ShapeDtypeStruct((rows.n, d), F32),
        scratch_shapes=[pltpu.VMEM((2, TOP_K, tm, d), F32), pltpu.SemaphoreType.DMA((2,))],
        compiler_params=_params("arbitrary"),
        name="moe_combine",
    )(*args)


def _moe_layer(layer, rows_p, rows_s, xp, xs, g, mod_p, mod_s, w_r, b_r, w1, b1, w2, b2, final_g, tmb):
    d = rows_p.d
    n_exp = w_r.shape[1]
    zero = jnp.zeros((1, LANES), F32)
    hp, idx_p, gates_p, rank_p, cnt = _router(rows_p, xp, g, mod_p[3], mod_p[4], w_r, b_r, zero)
    hs, idx_s, gates_s, rank_s, cnt = _router(rows_s, xs, g, mod_s[3], mod_s[4], w_r, b_r, cnt)
    n = rows_p.n + rows_s.n
    counts = cnt[0, :n_exp].astype(jnp.int32)
    padded = (counts + tmb - 1) // tmb * tmb
    pad_ends = jnp.cumsum(padded)
    pad_starts = pad_ends - padded
    lane_e = jnp.arange(n_exp, dtype=jnp.int32)

    def slots(idx, rank):
        starts = jnp.sum(jnp.where(idx[:, :TOP_K, None] == lane_e, pad_starts, 0), axis=-1)
        return (starts + rank[:, :TOP_K]).reshape(-1)

    dest_p = slots(idx_p, rank_p)
    dest_s = slots(idx_s, rank_s)
    n_blocks = -(-n * TOP_K // tmb) + n_exp
    block_start = jnp.arange(n_blocks, dtype=jnp.int32) * tmb
    block_expert = jnp.minimum(jnp.sum((pad_ends[None, :] <= block_start[:, None]).astype(jnp.int32), axis=1),
                               n_exp - 1)
    n_used = (pad_ends[-1:] // tmb).astype(jnp.int32)
    x_sorted = jnp.zeros((n_blocks * tmb, d), F32)
    x_sorted = _dispatch(rows_p, hp, dest_p, x_sorted)
    x_sorted = _dispatch(rows_s, hs, dest_s, x_sorted)
    y_slots = _experts(x_sorted, block_expert, n_used, w1, b1, w2, b2, layer, tmb)
    xp = _combine(rows_p, xp, mod_p[5], y_slots, dest_p, gates_p, final_g)
    xs = _combine(rows_s, xs, mod_s[5], y_slots, dest_s, gates_s, final_g)
    return xp, xs


def _proj_res_kernel(a_ref, w_ref, x_ref, gt_ref, o_ref):
    o_ref[...] = x_ref[...] + gt_ref[...] * jnp.dot(a_ref[...], w_ref[...], preferred_element_type=F32)


def _proj_residual(rows, a, w, x, gate):
    d = rows.d
    kdim = w.shape[0]
    return pl.pallas_call(
        _proj_res_kernel,
        grid=(rows.grid,),
        in_specs=[rows.x_spec(kdim), _const_spec((kdim, d)), rows.x_spec(), rows.mod_spec()],
        out_specs=rows.x_spec(),
        out_shape=jax.ShapeDtypeStruct((rows.n, d), F32),
        compiler_params=_params("arbitrary"),
        name="proj_residual",
    )(a, w.astype(BF16), x, rows.mod(gate))


def _nt_dot(a, b, precision=None):
    return lax.dot_general(a, b, (((1,), (1,)), ((), ())), precision=precision, preferred_element_type=F32)


def _fox_proj_kernel(seq_tiles, group, x_ref, g_ref, sh_ref, sc_ref, wqkv_ref, wf_ref, bf_ref,
                     q_ref, k_ref, v_ref, lf_ref, cum_ref, carry_ref):
    i = pl.program_id(0)
    tm, d = x_ref.shape
    h = _norm_mod(x_ref[...], g_ref[...], sh_ref[...], sc_ref[...])
    qkv = _bdot(h, wqkv_ref[...])
    q_ref[...] = qkv[:, :d]
    k_ref[...] = qkv[:, d:2 * d]
    v_ref[...] = qkv[:, 2 * d:]
    z = jnp.dot(h, wf_ref[...], precision=HIGHEST, preferred_element_type=F32) + bf_ref[...]
    lf = jnp.minimum(z, 0.0) - jnp.log1p(jnp.exp(-jnp.abs(z)))
    lf_ref[...] = lf
    r = lax.broadcasted_iota(jnp.int32, (tm, tm), 0)
    c = lax.broadcasted_iota(jnp.int32, (tm, tm), 1)
    if seq_tiles:
        @pl.when(i % seq_tiles == 0)
        def _():
            carry_ref[...] = jnp.zeros_like(carry_ref)

        tri = jnp.where(c <= r, 1.0, 0.0)
        cum_ref[...] = jnp.dot(tri, lf, precision=HIGHEST, preferred_element_type=F32) + carry_ref[...]
        carry_ref[...] = carry_ref[...] + jnp.sum(lf, axis=0, keepdims=True)
    else:
        tri = jnp.where(jnp.logical_and(c <= r, r // group == c // group), 1.0, 0.0)
        cum_ref[...] = jnp.dot(tri, lf, precision=HIGHEST, preferred_element_type=F32)


def _fox_proj(rows, x, g, shift, scale, w_qkv, w_f, b_f):
    d = rows.d
    nh = w_f.shape[1]
    wf = jnp.pad(w_f, ((0, 0), (0, LANES - nh)))
    bf = jnp.pad(b_f, (0, LANES - nh)).reshape(1, LANES)
    seq_tiles = rows.seq_len // rows.tm if rows.per_seq else 0
    lane_spec = pl.BlockSpec((rows.tm, LANES), lambda i: (i, 0))
    return pl.pallas_call(
        functools.partial(_fox_proj_kernel, seq_tiles, rows.seq_len),
        grid=(rows.grid,),
        in_specs=[rows.x_spec(), _const_spec((1, d)), rows.mod_spec(), rows.mod_spec(),
                  _const_spec((d, 3 * d)), _const_spec((d, LANES)), _const_spec((1, LANES))],
        out_specs=[rows.x_spec(), rows.x_spec(), rows.x_spec(), lane_spec, lane_spec],
        out_shape=[jax.ShapeDtypeStruct((rows.n, d), F32)] * 3 + [jax.ShapeDtypeStruct((rows.n, LANES), F32)] * 2,
        scratch_shapes=[pltpu.VMEM((1, LANES), F32)],
        compiler_params=_params("arbitrary"),
        name="fox_proj",
    )(x, g.reshape(1, d), rows.mod(shift), rows.mod(scale), w_qkv.astype(BF16), wf, bf)


def _flash_kernel(qi_ref, ki_ref, q_ref, k_ref, v_ref, o_ref, m_ref, l_ref, acc_ref):
    p_id = pl.program_id(1)
    qi = qi_ref[p_id]
    ki = ki_ref[p_id]
    tq = q_ref.shape[1]
    tk = k_ref.shape[1]
    nh = q_ref.shape[2] // LANES

    @pl.when(ki == 0)
    def _():
        m_ref[...] = jnp.full(m_ref.shape, -jnp.inf, F32)
        l_ref[...] = jnp.zeros(l_ref.shape, F32)
        acc_ref[...] = jnp.zeros(acc_ref.shape, F32)

    def scores(h):
        sl = slice(h * LANES, (h + 1) * LANES)
        return _nt_dot(q_ref[0, :, sl], k_ref[0, :, sl])

    def body(masked):
        if masked:
            keep = (lax.broadcasted_iota(jnp.int32, (tq, tk), 1) <= lax.broadcasted_iota(jnp.int32, (tq, tk), 0))
        s_next = scores(0)
        for h in range(nh):
            sl = slice(h * LANES, (h + 1) * LANES)
            s = s_next
            if h + 1 < nh:
                s_next = scores(h + 1)
            if masked:
                s = jnp.where(keep, s, -jnp.inf)
            m_prev = m_ref[h]
            m_new = jnp.maximum(m_prev, jnp.max(s, axis=-1, keepdims=True))
            corr = jnp.exp(m_prev - m_new)
            p = jnp.exp(s - jnp.tile(m_new, (1, tk // LANES)))
            l_ref[h] = l_ref[h] * corr + jnp.sum(p, axis=-1, keepdims=True)
            acc_ref[:, sl] = acc_ref[:, sl] * corr + jnp.dot(p.astype(BF16), v_ref[0, :, sl],
                                                            preferred_element_type=F32)
            m_ref[h] = m_new

    @pl.when(ki < qi)
    def _():
        body(False)

    @pl.when(ki == qi)
    def _():
        body(True)
        for h in range(nh):
            sl = slice(h * LANES, (h + 1) * LANES)
            o_ref[0, :, sl] = (acc_ref[:, sl] / l_ref[h]).astype(o_ref.dtype)


def _bf16_parts(c):
    def top16(x):
        bits = lax.bitcast_convert_type(x, jnp.uint32) & jnp.uint32(0xFFFF0000)
        return lax.bitcast_convert_type(bits, F32)

    c1 = top16(c)
    r1 = c - c1
    c2 = top16(r1)
    return c1, c2, top16(r1 - c2)


def _fox_head_layout(q, k, v, cum, nh):
    b, t, d = q.shape
    hd = d // nh
    parts = [p[..., None] for p in _bf16_parts(cum[:, :, :nh])]
    one = jnp.ones((b, t, nh, 1), F32)
    fill = jnp.zeros((b, t, nh, LANES - hd - 6), F32)
    qh = (q * hd ** -0.5).reshape(b, t, nh, hd)
    q_l = jnp.concatenate([qh] + parts + [one] * 3 + [fill], axis=-1)
    k_l = jnp.concatenate([k.reshape(b, t, nh, hd)] + [one] * 3 + [-p for p in parts] + [fill], axis=-1)
    v_l = jnp.pad(v.reshape(b, t, nh, hd), ((0, 0), (0, 0), (0, 0), (0, LANES - hd)))
    return tuple(z.astype(BF16).reshape(b, t, nh * LANES) for z in (q_l, k_l, v_l))


def _fox_flash(q_l, k_l, v_l, tile):
    b, t, dl = q_l.shape
    nh = dl // LANES
    tq = _tile(t, tile)
    nq = t // tq
    pairs = [(a, c) for a in range(nq) for c in range(a + 1)]
    qi = jnp.array([p[0] for p in pairs], jnp.int32)
    ki = jnp.array([p[1] for p in pairs], jnp.int32)
    q_spec = pl.BlockSpec((1, tq, dl), lambda bi, p, qi, ki: (bi, qi[p], 0))
    k_spec = pl.BlockSpec((1, tq, dl), lambda bi, p, qi, ki: (bi, ki[p], 0))
    grid_spec = pltpu.PrefetchScalarGridSpec(
        num_scalar_prefetch=2,
        grid=(b, len(pairs)),
        in_specs=[q_spec, k_spec, k_spec],
        out_specs=q_spec,
        scratch_shapes=[pltpu.VMEM((nh, tq, LANES), F32), pltpu.VMEM((nh, tq, LANES), F32),
                        pltpu.VMEM((tq, dl), F32)],
    )
    return pl.pallas_call(
        _flash_kernel,
        grid_spec=grid_spec,
        out_shape=jax.ShapeDtypeStruct((b, t, dl), BF16),
        compiler_params=_params("arbitrary", "arbitrary"),
        name="fox_flash",
    )(qi, ki, q_l, k_l, v_l)


def _paged_kernel(n_steps, nh, per_step, pt_ref, q_ref, kn_ref, vn_ref, g_ref, *rest):
    kc_refs, vc_refs, lfc_refs = (rest[j * per_step:(j + 1) * per_step] for j in range(3))
    o_ref, qbd_ref, acc_ref, m_ref, l_ref, carry_ref, lfp_ref = rest[3 * per_step:]
    s_id = pl.program_id(1)
    dseq, d = q_ref.shape
    page = kc_refs[0].shape[1]
    rws = nh * dseq
    hd = d // nh
    row_h = lax.broadcasted_iota(jnp.int32, (rws, LANES), 0) // dseq
    lane = lax.broadcasted_iota(jnp.int32, (rws, LANES), 1)
    head_sel = jnp.where(lane == row_h, 1.0, 0.0)
    g_new = g_ref[...]
    gq = jnp.sum(head_sel * jnp.concatenate([g_new] * nh, axis=0), axis=-1, keepdims=True)

    def online(s, v_bf):
        m_prev = m_ref[...]
        m_new = jnp.maximum(m_prev, jnp.max(s, axis=-1, keepdims=True))
        corr = jnp.exp(m_prev - m_new)
        p = jnp.exp(s - jnp.tile(m_new, (1, s.shape[1] // LANES)))
        l_ref[...] = l_ref[...] * corr + jnp.sum(p, axis=-1, keepdims=True)
        acc_ref[...] = acc_ref[...] * jnp.tile(corr, (1, d // LANES)) + jnp.dot(p.astype(BF16), v_bf,
                                                                                preferred_element_type=F32)
        m_ref[...] = m_new

    @pl.when(s_id == 0)
    def _():
        on_head = (lax.broadcasted_iota(jnp.int32, (rws, d), 0) // dseq
                   == lax.broadcasted_iota(jnp.int32, (rws, d), 1) // hd)
        q_rows = jnp.concatenate([q_ref[...] * (hd ** -0.5)] * nh, axis=0)
        qbd_ref[...] = jnp.where(on_head, q_rows, 0.0).astype(BF16)
        m_ref[...] = jnp.full(m_ref.shape, -jnp.inf, F32)
        l_ref[...] = jnp.zeros(l_ref.shape, F32)
        acc_ref[...] = jnp.zeros(acc_ref.shape, F32)
        carry_ref[...] = jnp.zeros(carry_ref.shape, F32)
        lfp_ref[...] = jnp.zeros(lfp_ref.shape, F32)
        pad = jnp.zeros((page - dseq, d), F32)
        k_pad = jnp.concatenate([kn_ref[...], pad], axis=0).astype(BF16)
        v_pad = jnp.concatenate([vn_ref[...], pad], axis=0).astype(BF16)
        g_pad = jnp.concatenate([g_new, jnp.zeros((page - dseq, LANES), F32)], axis=0)
        gk = _nt_dot(head_sel, g_pad, HIGHEST)
        t_row = lax.broadcasted_iota(jnp.int32, (rws, page), 0) % dseq
        u_col = lax.broadcasted_iota(jnp.int32, (rws, page), 1)
        s = _nt_dot(qbd_ref[...], k_pad) + (gq - gk)
        online(jnp.where(u_col <= t_row, s, -jnp.inf), v_pad)

    @pl.when(s_id > 0)
    def _():
        pj = lax.broadcasted_iota(jnp.int32, (page, page), 0)
        jj = lax.broadcasted_iota(jnp.int32, (page, page), 1)
        later = jnp.where(jj > pj, 1.0, 0.0)
        carry = carry_ref[...]
        biases = []
        for j in range(per_step):
            lfp_ref[j, :, 0:nh] = lfc_refs[j][0]
            lfp = lfp_ref[j]
            tail_t = jnp.dot(later, lfp, precision=HIGHEST, preferred_element_type=F32) + carry
            biases.append(_nt_dot(head_sel, tail_t, HIGHEST) + gq)
            carry = carry + jnp.sum(lfp, axis=0, keepdims=True)
        carry_ref[...] = carry
        q_bd = qbd_ref[...]
        s = jnp.concatenate([_nt_dot(q_bd, kc_refs[j][0].astype(BF16)) + biases[j] for j in range(per_step)], axis=1)
        online(s, jnp.concatenate([vc_refs[j][0].astype(BF16) for j in range(per_step)], axis=0))

    @pl.when(s_id == n_steps - 1)
    def _():
        on_head = (lax.broadcasted_iota(jnp.int32, (rws, d), 0) // dseq
                   == lax.broadcasted_iota(jnp.int32, (rws, d), 1) // hd)
        o_full = jnp.where(on_head, acc_ref[...] / jnp.tile(l_ref[...], (1, d // LANES)), 0.0)
        o_ref[...] = jnp.sum(o_full.reshape(nh, dseq, d), axis=0).astype(o_ref.dtype)


def _fox_paged(q, k_new, v_new, cum_new, cache_k, cache_v, cache_lf, page_table, dseq, layer):
    ns, d = q.shape
    bsz, n_pages = page_table.shape
    n_layers, pool, page, nh = cache_lf.shape
    rws = nh * dseq
    per_step = PAGES_PER_STEP if n_pages % PAGES_PER_STEP == 0 else 1
    n_steps = n_pages // per_step + 1

    def page_spec(j, last):
        def index(bi, s, pt):
            logical = n_pages - 1 - ((jnp.maximum(s, 1) - 1) * per_step + j)
            return (layer * pool + pt[bi * n_pages + logical], 0, 0)
        return pl.BlockSpec((1, page, last), index)

    seq_spec = pl.BlockSpec((dseq, d), lambda bi, s, pt: (bi, 0))
    grid_spec = pltpu.PrefetchScalarGridSpec(
        num_scalar_prefetch=1,
        grid=(bsz, n_steps),
        in_specs=([seq_spec, seq_spec, seq_spec, pl.BlockSpec((dseq, LANES), lambda bi, s, pt: (bi, 0))]
                  + [page_spec(j, d) for j in range(per_step)] * 2 + [page_spec(j, nh) for j in range(per_step)]),
        out_specs=seq_spec,
        scratch_shapes=[pltpu.VMEM((rws, d), BF16), pltpu.VMEM((rws, d), F32), pltpu.VMEM((rws, LANES), F32),
                        pltpu.VMEM((rws, LANES), F32), pltpu.VMEM((1, LANES), F32),
                        pltpu.VMEM((per_step, page, LANES), F32)],
    )
    kc = cache_k.reshape(n_layers * pool, page, d)
    vc = cache_v.reshape(n_layers * pool, page, d)
    lfc = cache_lf.reshape(n_layers * pool, page, nh)
    return pl.pallas_call(
        functools.partial(_paged_kernel, n_steps, nh, per_step),
        grid_spec=grid_spec,
        out_shape=jax.ShapeDtypeStruct((ns, d), BF16),
        compiler_params=_params("arbitrary", "arbitrary"),
        name="fox_paged",
    )(page_table.reshape(-1), q, k_new, v_new, cum_new, *([kc] * per_step), *([vc] * per_step), *([lfc] * per_step))


def _fox_layer(layer, rows_p, rows_s, xp, xs, g, mod_p, mod_s, cache_k, cache_v, cache_lf, page_table,
               w_qkv, w_f, b_f, w_o, flash_tile):
    d = rows_p.d
    nh = w_f.shape[1]
    bp, t = rows_p.n_seq, rows_p.seq_len
    qp, kp, vp, lfp, cump = _fox_proj(rows_p, xp, g, mod_p[0], mod_p[1], w_qkv, w_f, b_f)
    qs, ks, vs, lfs, cums = _fox_proj(rows_s, xs, g, mod_s[0], mod_s[1], w_qkv, w_f, b_f)
    q_l, k_l, v_l = _fox_head_layout(qp.reshape(bp, t, d), kp.reshape(bp, t, d), vp.reshape(bp, t, d),
                                     cump.reshape(bp, t, LANES), nh)
    op = _fox_flash(q_l, k_l, v_l, flash_tile)
    os_ = _fox_paged(qs, ks, vs, cums, cache_k, cache_v, cache_lf, page_table, rows_s.seq_len, layer)
    hd = d // nh
    w_o_l = jnp.pad(w_o.reshape(nh, hd, d), ((0, 0), (0, LANES - hd), (0, 0))).reshape(nh * LANES, d)
    xp = _proj_residual(rows_p, op.reshape(-1, nh * LANES), w_o_l, xp, mod_p[2])
    xs = _proj_residual(rows_s, os_, w_o, xs, mod_s[2])
    return xp, xs, (kp, vp, lfp[:, :nh]), (ks, vs, lfs[:, :nh])


def _log_sigmoid(z):
    return jnp.minimum(z, 0.0) - jnp.log1p(jnp.exp(-jnp.abs(z)))


def _head_sum(x, hd):
    i = lax.broadcasted_iota(jnp.int32, (LANES, LANES), 0) // hd
    j = lax.broadcasted_iota(jnp.int32, (LANES, LANES), 1) // hd
    same = jnp.where(i == j, 1.0, 0.0)
    parts = [jnp.dot(x[:, c * LANES:(c + 1) * LANES], same, precision=HIGHEST, preferred_element_type=F32)
             for c in range(x.shape[1] // LANES)]
    return jnp.concatenate(parts, axis=1)


def _rwkv_proj_kernel(seq_tiles, group, x_ref, xprev_ref, sh_ref, g_ref, shift_ref, scale_ref, mu_ref, wrkv_ref,
                      w0_ref, w1_ref, w2_ref, a0_ref, a1_ref, a2_ref, g1_ref, g2_ref, kk_ref, ka_ref, rk_ref,
                      r_out, lw_out, k_out, v_out, a_out, b_out, bonus_out, gate_out, h_out):
    i = pl.program_id(0)
    tm, d = x_ref.shape
    gvec, shift, scale = g_ref[...], shift_ref[...], scale_ref[...]
    h = _norm_mod(x_ref[...], gvec, shift, scale)
    rolled = pltpu.roll(h, 1, 0)
    row = lax.broadcasted_iota(jnp.int32, (tm, d), 0)
    if seq_tiles:
        h_before = _norm_mod(xprev_ref[...], gvec, shift, scale)[7:8]
        first = jnp.where(i % seq_tiles == 0, sh_ref[...], h_before)
        x_prev = jnp.where(row == 0, first, rolled)
        h_out[...] = h[tm - 8:]
    else:
        x_prev = jnp.where(row % group == 0, sh_ref[...], rolled)
        h_out[...] = h
    xx = x_prev - h
    xr, xw, xk, xv, xa, xg = (h + xx * mu_ref[n:n + 1] for n in range(6))
    r = _bdot(xr, wrkv_ref[0])
    k = _bdot(xk, wrkv_ref[1])
    v = _bdot(xv, wrkv_ref[2])
    w_log = _log_sigmoid(w0_ref[...] + _bdot(jnp.tanh(_bdot(xw, w1_ref[...])), w2_ref[...])) - 0.5
    iclr = jax.nn.sigmoid(a0_ref[...] + _bdot(_bdot(xa, a1_ref[...]), a2_ref[...]))
    gate_out[...] = _bdot(jax.nn.sigmoid(_bdot(xg, g1_ref[...])), g2_ref[...])
    kk = k * kk_ref[...]
    kk = kk / jnp.maximum(jnp.sqrt(_head_sum(kk * kk, RW_HEAD)), 1e-12)
    k2 = k * (1.0 + (iclr - 1.0) * ka_ref[...])
    r_out[...] = r
    lw_out[...] = -jnp.exp(w_log)
    k_out[...] = k2
    v_out[...] = v
    a_out[...] = -kk
    b_out[...] = kk * iclr
    bonus_out[...] = _head_sum(r * k2 * rk_ref[...], RW_HEAD) * v


def _pad_to(x, axis, size):
    pad = [(0, 0)] * x.ndim
    pad[axis] = (0, size - x.shape[axis])
    return jnp.pad(x, pad)


def _rwkv_proj(rows, x, g, shift, scale, shift_state, mu, w_rkv, w0, w1, w2, a0, a1, a2, g1, g2, k_k, k_a, r_k):
    d = rows.d
    lw_, la_, lg_ = (-(-w.shape[1] // LANES) * LANES for w in (w1, a1, g1))
    seq_tiles = rows.seq_len // rows.tm if rows.per_seq else 0
    if rows.per_seq:
        per8 = rows.tm // 8
        xprev_spec = pl.BlockSpec((8, d), lambda i: (jnp.maximum(i * per8 - 1, 0), 0))
        h_shape = jax.ShapeDtypeStruct((rows.grid * 8, d), F32)
        h_spec = pl.BlockSpec((8, d), lambda i: (i, 0))
    else:
        xprev_spec = pl.BlockSpec((8, d), lambda i: (0, 0))
        h_shape = jax.ShapeDtypeStruct((rows.n, d), F32)
        h_spec = rows.x_spec()
    vec = _const_spec((1, d))
    out = pl.pallas_call(
        functools.partial(_rwkv_proj_kernel, seq_tiles, rows.seq_len),
        grid=(rows.grid,),
        in_specs=[rows.x_spec(), xprev_spec, rows.mod_spec(), vec, rows.mod_spec(), rows.mod_spec(),
                  _const_spec((6, d)), _const_spec((3, d, d)),
                  vec, _const_spec((d, lw_)), _const_spec((lw_, d)),
                  vec, _const_spec((d, la_)), _const_spec((la_, d)),
                  _const_spec((d, lg_)), _const_spec((lg_, d)), vec, vec, vec],
        out_specs=[rows.x_spec()] * 8 + [h_spec],
        out_shape=[jax.ShapeDtypeStruct((rows.n, d), F32)] * 8 + [h_shape],
        compiler_params=_params("arbitrary"),
        name="rwkv_proj",
    )(x, x, rows.mod(shift_state), g.reshape(1, d), rows.mod(shift), rows.mod(scale), mu, w_rkv.astype(BF16),
      w0.reshape(1, d), _pad_to(w1, 1, lw_).astype(BF16), _pad_to(w2, 0, lw_).astype(BF16),
      a0.reshape(1, d), _pad_to(a1, 1, la_).astype(BF16), _pad_to(a2, 0, la_).astype(BF16),
      _pad_to(g1, 1, lg_).astype(BF16), _pad_to(g2, 0, lg_).astype(BF16),
      k_k.reshape(1, d), k_a.reshape(1, d), r_k.reshape(1, d))
    return out


def _tn_dot(a, b):
    return lax.dot_general(a.astype(BF16), b.astype(BF16), (((0,), (0,)), ((), ())), preferred_element_type=F32)


def _wkv_kernel(chunk, r_ref, lw_ref, k_ref, v_ref, a_ref, b_ref, s0_ref, y_ref, sT_ref, st_ref):
    tt_id = pl.program_id(2)
    nseq, tt, width = r_ref.shape
    hd = RW_HEAD
    nhead = width // hd
    ln = chunk
    nchunk = tt // ln

    @pl.when(tt_id == 0)
    def _():
        st_ref[...] = s0_ref[...]

    rr = lax.broadcasted_iota(jnp.int32, (ln, ln), 0)
    cc = lax.broadcasted_iota(jnp.int32, (ln, ln), 1)
    tri = jnp.where(cc <= rr, 1.0, 0.0)
    eye = jnp.where(cc == rr, 1.0, 0.0)
    r2 = lax.broadcasted_iota(jnp.int32, (2 * ln, 2 * ln), 0)
    c2 = lax.broadcasted_iota(jnp.int32, (2 * ln, 2 * ln), 1) % ln
    keep = c2 < jnp.where(r2 < ln, r2, r2 - ln + 1)
    n_double = ln.bit_length() - 2
    items = [(s, c, j) for s in range(nseq) for c in range(nchunk) for j in range(nhead)]

    scaled = {}
    for s in range(nseq):
        for c in range(nchunk):
            rows = slice(c * ln, (c + 1) * ln)
            lw = lw_ref[s, rows, :]
            cw = jnp.dot(tri, lw, precision=HIGHEST, preferred_element_type=F32)
            e_neg = jnp.exp(-cw)
            e_all = jnp.exp(cw[ln - 1:ln, :])
            b_t = b_ref[s, rows, :] * e_neg
            k_t = k_ref[s, rows, :] * e_neg
            scaled[s, c] = (a_ref[s, rows, :] * jnp.exp(cw - lw), r_ref[s, rows, :] * jnp.exp(cw), b_t, k_t,
                            b_t * e_all, k_t * e_all, e_all, v_ref[s, rows, :])

    gram, a_t, r_t, vj, ends, e_all = {}, {}, {}, {}, {}, {}
    for it in items:
        s, c, j = it
        sl = slice(j * hd, (j + 1) * hd)
        at, rt, bt, kt, b_end, k_end, ea, vv = scaled[s, c]
        a_t[it], r_t[it], vj[it], e_all[it] = at[:, sl], rt[:, sl], vv[:, sl], ea[:, sl]
        ends[it] = jnp.concatenate([b_end[:, sl], k_end[:, sl]], axis=0)
        lhs = jnp.concatenate([at[:, sl], rt[:, sl]], axis=0)
        rhs = jnp.concatenate([bt[:, sl], kt[:, sl]], axis=0)
        gram[it] = jnp.where(keep, _nt_dot(lhs.astype(BF16), rhs.astype(BF16)), 0.0)
    inv = {it: eye + gram[it][:ln, :ln] for it in items}
    power = {it: gram[it][:ln, :ln] for it in items}
    for _ in range(n_double):
        power = {it: _bdot(power[it], power[it]) for it in items}
        inv = {it: inv[it] + _bdot(inv[it], power[it]) for it in items}
    x0 = {it: _bdot(gram[it][:ln, ln:], vj[it]) for it in items}
    gu = {it: _bdot(inv[it], jnp.concatenate([a_t[it], x0[it]], axis=1)) for it in items}
    uv = {it: jnp.concatenate([gu[it][:, hd:], vj[it]], axis=0) for it in items}
    q_r = {it: r_t[it] + _bdot(gram[it][ln:, :ln], gu[it][:, :hd]) for it in items}
    y0 = {it: _bdot(gram[it][ln:, :], uv[it]) for it in items}
    m_st = {it: _tn_dot(gu[it][:, :hd], ends[it][:ln]) for it in items}
    c_st = {it: _tn_dot(uv[it], ends[it]) for it in items}

    for s in range(nseq):
        states = [st_ref[s, j] for j in range(nhead)]
        for c in range(nchunk):
            ys = []
            for j in range(nhead):
                it = (s, c, j)
                ys.append(_nt_dot(q_r[it].astype(BF16), states[j].astype(BF16)) + y0[it])
                states[j] = states[j] * e_all[it] + _bdot(states[j], m_st[it]) + c_st[it]
            y_ref[s, c * ln:(c + 1) * ln, :] = jnp.concatenate(ys, axis=1)
        for j in range(nhead):
            st_ref[s, j] = states[j]

    @pl.when(tt_id == pl.num_programs(2) - 1)
    def _():
        sT_ref[...] = st_ref[...]


def _wkv(r, lw, k, v, a, b, state0, chunk, tile, nseq, width):
    bsz, t, d = r.shape
    tt = _tile(t, tile)
    assert tt % chunk == 0 and bsz % nseq == 0 and d % width == 0
    nhead = width // RW_HEAD
    seq_spec = pl.BlockSpec((nseq, tt, width), lambda bi, hp, ti: (bi, ti, hp))
    st_spec = pl.BlockSpec((nseq, nhead, RW_HEAD, RW_HEAD), lambda bi, hp, ti: (bi, hp, 0, 0))
    return pl.pallas_call(
        functools.partial(_wkv_kernel, chunk),
        grid=(bsz // nseq, d // width, t // tt),
        in_specs=[seq_spec] * 6 + [st_spec],
        out_specs=[seq_spec, st_spec],
        out_shape=[jax.ShapeDtypeStruct((bsz, t, d), F32), jax.ShapeDtypeStruct(state0.shape, F32)],
        scratch_shapes=[pltpu.VMEM((nseq, nhead, RW_HEAD, RW_HEAD), F32)],
        compiler_params=_params("arbitrary", "arbitrary", "arbitrary"),
        name="wkv",
    )(r, lw, k, v, a, b, state0)


def _rwkv_out_kernel(y_ref, bonus_ref, gate_ref, x_ref, gt_ref, lng_ref, lnb_ref, wo_ref, o_ref):
    y = y_ref[...]
    mean = _head_sum(y, RW_HEAD) * (1.0 / RW_HEAD)
    yc = y - mean
    var = _head_sum(yc * yc, RW_HEAD) * (1.0 / RW_HEAD)
    yn = yc * lax.rsqrt(var + RW_GN_EPS) * lng_ref[...] + lnb_ref[...] + bonus_ref[...]
    o_ref[...] = x_ref[...] + gt_ref[...] * _bdot(yn * gate_ref[...], wo_ref[...])


def _rwkv_out(rows, y, bonus, gate, x, gate1, ln_g, ln_b, w_o):
    d = rows.d
    return pl.pallas_call(
        _rwkv_out_kernel,
        grid=(rows.grid,),
        in_specs=[rows.x_spec()] * 4 + [rows.mod_spec(), _const_spec((1, d)), _const_spec((1, d)), _const_spec((d, d))],
        out_specs=rows.x_spec(),
        out_shape=jax.ShapeDtypeStruct((rows.n, d), F32),
        compiler_params=_params("arbitrary"),
        name="rwkv_out",
    )(y, bonus, gate, x, rows.mod(gate1), ln_g.reshape(1, d), ln_b.reshape(1, d), w_o.astype(BF16))


def _rwkv_group(rows, x, g, mod, shift_state, wkv_state, mu, w_rkv, w_o, w0, w1, w2, a0, a1, a2, g1, g2,
                k_k, k_a, r_k, ln_g, ln_b, chunk, tile, nseq):
    d = rows.d
    bsz, t = rows.n_seq, rows.seq_len
    r, lw, k, v, a, b, bonus, gate, h_tail = _rwkv_proj(rows, x, g, mod[0], mod[1], shift_state, mu, w_rkv,
                                                        w0, w1, w2, a0, a1, a2, g1, g2, k_k, k_a, r_k)
    to3 = lambda z: z.reshape(bsz, t, d)
    y, state = _wkv(to3(r), to3(lw), to3(k), to3(v), to3(a), to3(b), wkv_state, chunk, tile, nseq, WKV_LANES)
    x_new = _rwkv_out(rows, y.reshape(-1, d), bonus, gate, x, mod[2], ln_g, ln_b, w_o)
    if rows.per_seq:
        h_last = h_tail.reshape(bsz, -1, d)[:, -1]
    else:
        h_last = h_tail.reshape(bsz, t, d)[:, -1]
    return x_new, h_last, state


def kernel(x_prompt, x_sample, cache_fox_k, cache_fox_v, cache_fox_logf, state_rwkv_shift, state_rwkv_wkv, page_table, c_prompt, c_sample, norm1_g, norm2_g, ada_w, ada_b, cm_w_in, cm_b_in, cm_ln_g, cm_ln_b, cm_w_s, cm_b_s, cm_w_out, cm_b_out, rw_mu, rw_w_rkv, rw_w_o, rw_w0, rw_w1, rw_w2, rw_a0, rw_a1, rw_a2, rw_g1, rw_g2, rw_k_k, rw_k_a, rw_r_k, rw_ln_g, rw_ln_b, fox_w_qkv, fox_w_f, fox_b_f, fox_w_o, moe_wr, moe_br, moe_w1, moe_b1, moe_w2, moe_b2, final_g):
    bp, seq, d = x_prompt.shape
    bs, dseq, _ = x_sample.shape
    c_all = jnp.concatenate([c_prompt, c_sample], 0)
    n = c_all.shape[0]
    npad = -(-n // 8) * 8
    mod = _adaln(jnp.pad(c_all, ((0, npad - n), (0, 0))), ada_w, ada_b)
    depth = ada_w.shape[0]
    rows_p = _Rows(bp, seq, d, ROW_TILE)
    rows_s = _Rows(bs, dseq, d, ROW_TILE)
    width = cm_w_out.shape[1]
    nh_rw = d // RW_HEAD
    nh_fox = fox_w_f.shape[2]
    xp = x_prompt.reshape(-1, d)
    xs = x_sample.reshape(-1, d)
    cm_v_s = []
    rw_sh_p, rw_wkv_p, rw_sh_s, rw_wkv_s = [], [], [], []
    fx_p, fx_s = [], []
    for i in range(depth):
        kind, j = i % N_MIXERS, i // N_MIXERS
        mp = jnp.split(mod[i, :bp], 6, axis=-1)
        ms = jnp.split(mod[i, bp:n], 6, axis=-1)
        if kind == 0:
            cm = (cm_w_in[j], cm_b_in[j], cm_ln_g[j], cm_ln_b[j])
            ws_p, bs_p = _cm_spatial_weights(cm_w_s[j], cm_b_s[j], CHUNK, width)
            ws_s, bs_s = _cm_spatial_weights(cm_w_s[j], cm_b_s[j], dseq, width)
            (xp,) = _cm_layer(rows_p, xp, norm1_g[i], mp[0], mp[1], mp[2], *cm, ws_p, bs_p, cm_w_out[j], cm_b_out[j], False)
            xs, v_new = _cm_layer(rows_s, xs, norm1_g[i], ms[0], ms[1], ms[2], *cm, ws_s, bs_s, cm_w_out[j], cm_b_out[j], True)
            cm_v_s.append(v_new.reshape(bs, dseq, width))
        elif kind == 1:
            rw = (rw_mu[j], rw_w_rkv[j], rw_w_o[j], rw_w0[j], rw_w1[j], rw_w2[j], rw_a0[j], rw_a1[j], rw_a2[j],
                  rw_g1[j], rw_g2[j], rw_k_k[j], rw_k_a[j], rw_r_k[j], rw_ln_g[j], rw_ln_b[j])
            xp, sh_p, wkv_p = _rwkv_group(rows_p, xp, norm1_g[i], mp, jnp.zeros((bp, d), F32),
                                          jnp.zeros((bp, nh_rw, RW_HEAD, RW_HEAD), F32), *rw,
                                          min(WKV_CHUNK, seq), WKV_TILE, 1)
            xs, sh_s, wkv_s = _rwkv_group(rows_s, xs, norm1_g[i], ms, state_rwkv_shift[j], state_rwkv_wkv[j], *rw,
                                          dseq, dseq, min(WKV_DECODE_SEQS, bs))
            rw_sh_p.append(sh_p)
            rw_wkv_p.append(wkv_p)
            rw_sh_s.append(sh_s)
            rw_wkv_s.append(wkv_s)
        else:
            xp, xs, (k_p, v_p, lf_p), (k_s, v_s, lf_s) = _fox_layer(
                j, rows_p, rows_s, xp, xs, norm1_g[i], mp, ms, cache_fox_k, cache_fox_v, cache_fox_logf,
                page_table, fox_w_qkv[j], fox_w_f[j], fox_b_f[j], fox_w_o[j], FLASH_TILE)
            hd = d // nh_fox
            fx_p.append((k_p.reshape(bp, seq, nh_fox, hd), v_p.reshape(bp, seq, nh_fox, hd),
                         lf_p.reshape(bp, seq, nh_fox)))
            fx_s.append((k_s.reshape(bs, dseq, nh_fox, hd), v_s.reshape(bs, dseq, nh_fox, hd),
                         lf_s.reshape(bs, dseq, nh_fox)))
        xp, xs = _moe_layer(i, rows_p, rows_s, xp, xs, norm2_g[i], mp, ms, moe_wr[i], moe_br[i],
                            moe_w1, moe_b1, moe_w2, moe_b2, final_g if i == depth - 1 else None, MOE_ROWS)
    return (xp.reshape(bp, seq, d), xs.reshape(bs, dseq, d), jnp.stack(cm_v_s),
            jnp.stack(rw_sh_p), jnp.stack(rw_wkv_p), jnp.stack(rw_sh_s), jnp.stack(rw_wkv_s),
            jnp.stack([f[0] for f in fx_p]), jnp.stack([f[1] for f in fx_p]), jnp.stack([f[2] for f in fx_p]),
            jnp.stack([f[0] for f in fx_s]), jnp.stack([f[1] for f in fx_s]), jnp.stack([f[2] for f in fx_s]))
```

```python
import functools

import jax
import jax.numpy as jnp
from jax import lax
from jax.experimental import pallas as pl
from jax.experimental.pallas import tpu as pltpu

F32 = jnp.float32
BF16 = jnp.bfloat16
HIGHEST = lax.Precision.HIGHEST

NORM_EPS = 1e-6
N_MIXERS = 3
CHUNK = 128
CM_GROUPS = 8
RW_HEAD = 64
RW_GN_EPS = 64e-5
FOX_HEAD = 64
PAGE_SIZE = 128
TOP_K = 4
SWIGLU_LIMIT = 7.0
SWIGLU_ALPHA = 1.702
MOE_BLOCK = 128

V7X_VMEM_BYTES = 64 * 1024 * 1024
VMEM_LIMIT = V7X_VMEM_BYTES - 8 * 1024 * 1024
LANES = 128

ROW_TILE = 256
FLASH_TILE = 512
WKV_CHUNK = 64
WKV_TILE = 256
WKV_LANES = 256
WKV_DECODE_SEQS = 8
MOE_ROWS = 512
PAGES_PER_STEP = 4


def _params(*sem):
    return pltpu.CompilerParams(dimension_semantics=sem, vmem_limit_bytes=VMEM_LIMIT)


def _bdot(a, b):
    return jnp.dot(a.astype(BF16), b.astype(BF16), preferred_element_type=F32)


def _norm_mod(x, g, shift, scale):
    y = x * lax.rsqrt(jnp.mean(x * x, axis=-1, keepdims=True) + NORM_EPS)
    return (y * g) * (1.0 + scale) + shift


def _tile(n, want):
    t = min(n, want)
    assert n % t == 0, (n, want)
    return t


class _Rows:
    def __init__(self, n_seq, seq_len, d, want_tile):
        self.n_seq, self.seq_len, self.d = n_seq, seq_len, d
        self.n = n_seq * seq_len
        self.per_seq = seq_len >= want_tile or seq_len >= 128
        if self.per_seq:
            self.tm = _tile(seq_len, want_tile)
        else:
            self.tm = _tile(self.n, want_tile)
            assert self.tm % seq_len == 0
        self.grid = self.n // self.tm

    def x_spec(self, width=None):
        return pl.BlockSpec((self.tm, width or self.d), lambda i: (i, 0))

    def mod_spec(self):
        if self.per_seq:
            per = self.seq_len // self.tm
            return pl.BlockSpec((None, 1, self.d), lambda i: (i // per, 0, 0))
        return pl.BlockSpec((self.tm, self.d), lambda i: (i, 0))

    def mod(self, m):
        if self.per_seq:
            return m.reshape(self.n_seq, 1, self.d)
        return jnp.repeat(m, self.seq_len, axis=0)


def _const_spec(shape):
    nd = len(shape)
    return pl.BlockSpec(shape, lambda *_: (0,) * nd)


def _adaln_kernel(c_ref, w_ref, b_ref, o_ref):
    c = c_ref[...]
    s = c * jax.nn.sigmoid(c)
    o_ref[0] = _bdot(s, w_ref[0]) + b_ref[0]


def _adaln(c_all, ada_w, ada_b):
    depth, d, d6 = ada_w.shape
    n = c_all.shape[0]
    tn = _tile(d6, 1536)
    return pl.pallas_call(
        _adaln_kernel,
        grid=(depth, d6 // tn),
        in_specs=[
            pl.BlockSpec((n, d), lambda i, j: (0, 0)),
            pl.BlockSpec((1, d, tn), lambda i, j: (i, 0, j)),
            pl.BlockSpec((1, 1, tn), lambda i, j: (i, 0, j)),
        ],
        out_specs=pl.BlockSpec((1, n, tn), lambda i, j: (i, 0, j)),
        out_shape=jax.ShapeDtypeStruct((depth, n, d6), F32),
        compiler_params=_params("arbitrary", "arbitrary"),
        name="adaln",
    )(c_all, ada_w, ada_b.reshape(depth, 1, d6))


def _cm_kernel(emit_v, x_ref, g_ref, sh_ref, sc_ref, gt_ref, win_ref, bin_ref, lng_ref, lnb_ref, ws_ref, bs_ref,
               wout_ref, bout_ref, xo_ref, *rest):
    if emit_v:
        v_ref, gated_ref = rest
    else:
        (gated_ref,) = rest
    tm = x_ref.shape[0]
    width = lng_ref.shape[-1]
    gd = width // CM_GROUPS
    x = x_ref[...]
    h = _norm_mod(x, g_ref[...], sh_ref[...], sc_ref[...])
    z = jax.nn.gelu(_bdot(h, win_ref[...]) + bin_ref[...])
    u = z[:, :width]
    v = z[:, width:]
    mu = jnp.mean(v, axis=-1, keepdims=True)
    vc = v - mu
    var = jnp.mean(vc * vc, axis=-1, keepdims=True)
    vn = vc * lax.rsqrt(var + NORM_EPS) * lng_ref[...] + lnb_ref[...]
    if emit_v:
        v_ref[...] = vn
    vb = vn.astype(BF16)
    row = lax.broadcasted_iota(jnp.int32, (CHUNK, CHUNK), 0)
    col = lax.broadcasted_iota(jnp.int32, (CHUNK, CHUNK), 1)
    causal = col <= row
    for g in range(CM_GROUPS):
        wg = jnp.where(causal, ws_ref[g], 0.0).astype(BF16)
        for c in range(tm // CHUNK):
            r0 = c * CHUNK
            s = jnp.dot(wg, vb[r0:r0 + CHUNK, g * gd:(g + 1) * gd], preferred_element_type=F32)
            s = s + bs_ref[:, g * gd:(g + 1) * gd]
            gated_ref[r0:r0 + CHUNK, g * gd:(g + 1) * gd] = (u[r0:r0 + CHUNK, g * gd:(g + 1) * gd] * s).astype(BF16)
    y = jnp.dot(gated_ref[...], wout_ref[...], preferred_element_type=F32) + bout_ref[...]
    xo_ref[...] = x + gt_ref[...] * y


def _cm_layer(rows, x, g, shift, scale, gate, w_in, b_in, ln_g, ln_b, ws_bd, bs_rows, w_out, b_out, emit_v):
    d = rows.d
    width = w_out.shape[0]
    tm = rows.tm
    assert tm % CHUNK == 0
    out_shape = [jax.ShapeDtypeStruct((rows.n, d), F32)]
    out_specs = [rows.x_spec()]
    if emit_v:
        out_shape.append(jax.ShapeDtypeStruct((rows.n, width), F32))
        out_specs.append(rows.x_spec(width))
    res = pl.pallas_call(
        functools.partial(_cm_kernel, emit_v),
        grid=(rows.grid,),
        in_specs=[
            rows.x_spec(), _const_spec((1, d)), rows.mod_spec(), rows.mod_spec(), rows.mod_spec(),
            _const_spec((d, 2 * width)), _const_spec((1, 2 * width)), _const_spec((1, width)), _const_spec((1, width)),
            _const_spec((CM_GROUPS, CHUNK, CHUNK)), _const_spec((CHUNK, width)),
            _const_spec((width, d)), _const_spec((1, d)),
        ],
        out_specs=out_specs,
        out_shape=out_shape,
        scratch_shapes=[pltpu.VMEM((tm, width), BF16)],
        compiler_params=_params("arbitrary"),
        name="cm_layer_v" if emit_v else "cm_layer",
    )(x, g.reshape(1, d), rows.mod(shift), rows.mod(scale), rows.mod(gate),
      w_in.astype(BF16), b_in.reshape(1, -1), ln_g.reshape(1, -1), ln_b.reshape(1, -1),
      ws_bd, bs_rows, w_out.astype(BF16), b_out.reshape(1, d))
    return res


def _cm_spatial_weights(w_s, b_s, chunk_len, width):
    groups = w_s.shape[0]
    rep = CHUNK // chunk_len
    ws = w_s[:, :chunk_len, :chunk_len]
    if rep > 1:
        eye = jnp.eye(rep, dtype=w_s.dtype)
        ws = jnp.einsum("ab,grq->garbq", eye, ws).reshape(groups, CHUNK, CHUNK)
    bs = jnp.tile(b_s[:, :chunk_len].T, (rep, 1))
    return ws, jnp.repeat(bs, width // groups, axis=1)


NEG_BIG = -1e30


def _router_kernel(x_ref, g_ref, sh_ref, sc_ref, wr_ref, br_ref, cin_ref,
                   h_ref, idx_ref, gate_ref, rank_ref, cnt_ref):
    i = pl.program_id(0)
    tm = x_ref.shape[0]

    @pl.when(i == 0)
    def _():
        cnt_ref[...] = cin_ref[...]

    h = _norm_mod(x_ref[...], g_ref[...], sh_ref[...], sc_ref[...])
    h_ref[...] = h
    logits = jnp.dot(h, wr_ref[...], precision=HIGHEST, preferred_element_type=F32) + br_ref[...]
    col = lax.broadcasted_iota(jnp.int32, (tm, LANES), 1).astype(F32)
    work = logits
    hits = jnp.zeros((tm, LANES), F32)
    vals, sels = [], []
    for _ in range(TOP_K):
        m = jnp.max(work, axis=-1, keepdims=True)
        sel = jnp.min(jnp.where(work == m, col, float(LANES)), axis=-1, keepdims=True)
        hit = col == sel
        hits = hits + hit.astype(F32)
        work = jnp.where(hit, -jnp.inf, work)
        vals.append(m)
        sels.append(sel)
    es = [jnp.exp(v - vals[0]) for v in vals]
    tot = es[0] + es[1] + es[2] + es[3]
    r = lax.broadcasted_iota(jnp.int32, (tm, tm), 0)
    c = lax.broadcasted_iota(jnp.int32, (tm, tm), 1)
    below = jnp.where(c < r, 1.0, 0.0).astype(BF16)
    prefix = jnp.dot(below, hits.astype(BF16), preferred_element_type=F32) + cnt_ref[...]
    idx_o = jnp.zeros((tm, LANES), F32)
    gate_o = jnp.zeros((tm, LANES), F32)
    rank_o = jnp.zeros((tm, LANES), F32)
    for k in range(TOP_K):
        rank_k = jnp.sum(jnp.where(col == sels[k], prefix, 0.0), axis=-1, keepdims=True)
        idx_o = jnp.where(col == float(k), sels[k], idx_o)
        gate_o = jnp.where(col == float(k), es[k] / tot, gate_o)
        rank_o = jnp.where(col == float(k), rank_k, rank_o)
    idx_ref[...] = idx_o.astype(jnp.int32)
    gate_ref[...] = gate_o
    rank_ref[...] = rank_o.astype(jnp.int32)
    cnt_ref[...] = cnt_ref[...] + jnp.sum(hits, axis=0, keepdims=True)


def _router(rows, x, g, shift, scale, w_r, b_r, counts_in):
    d = rows.d
    n_exp = w_r.shape[1]
    wr = jnp.pad(w_r, ((0, 0), (0, LANES - n_exp)))
    br = jnp.pad(b_r, (0, LANES - n_exp), constant_values=NEG_BIG).reshape(1, LANES)
    lane_spec = pl.BlockSpec((rows.tm, LANES), lambda i: (i, 0))
    return pl.pallas_call(
        _router_kernel,
        grid=(rows.grid,),
        in_specs=[rows.x_spec(), _const_spec((1, d)), rows.mod_spec(), rows.mod_spec(),
                  _const_spec((d, LANES)), _const_spec((1, LANES)), _const_spec((1, LANES))],
        out_specs=[rows.x_spec(), lane_spec, lane_spec, lane_spec, _const_spec((1, LANES))],
        out_shape=[jax.ShapeDtypeStruct((rows.n, d), F32),
                   jax.ShapeDtypeStruct((rows.n, LANES), jnp.int32),
                   jax.ShapeDtypeStruct((rows.n, LANES), F32),
                   jax.ShapeDtypeStruct((rows.n, LANES), jnp.int32),
                   jax.ShapeDtypeStruct((1, LANES), F32)],
        compiler_params=_params("arbitrary"),
        name="moe_router",
    )(x, g.reshape(1, d), rows.mod(shift), rows.mod(scale), wr, br, counts_in)


def _experts_kernel(be_ref, nu_ref, x_ref, w1_ref, b1_ref, w2_ref, b2_ref, y_ref, w1b_ref, w2b_ref):
    i = pl.program_id(0)
    f = w2_ref.shape[2]
    new_expert = jnp.logical_or(i == 0, be_ref[i] != be_ref[jnp.maximum(i - 1, 0)])

    @pl.when(jnp.logical_and(new_expert, i < nu_ref[0]))
    def _():
        w1b_ref[...] = w1_ref[0, 0].astype(BF16)
        w2b_ref[...] = w2_ref[0, 0].astype(BF16)

    @pl.when(i < nu_ref[0])
    def _():
        hcat = jnp.dot(x_ref[...].astype(BF16), w1b_ref[...], preferred_element_type=F32) + b1_ref[0, 0]
        gate = jnp.minimum(hcat[:, :f], SWIGLU_LIMIT)
        up = jnp.clip(hcat[:, f:], -SWIGLU_LIMIT, SWIGLU_LIMIT)
        act = (up + 1.0) * (gate * jax.nn.sigmoid(SWIGLU_ALPHA * gate))
        y_ref[...] = jnp.dot(act.astype(BF16), w2b_ref[...], preferred_element_type=F32) + b2_ref[0, 0]

    @pl.when(i >= nu_ref[0])
    def _():
        y_ref[...] = jnp.zeros(y_ref.shape, F32)


def _experts(x_sorted, block_expert, n_used, w1, b1, w2, b2, layer, tmb):
    n_slots, d = x_sorted.shape
    _, n_exp, _, f2 = w1.shape
    f = f2 // 2
    n_blocks = n_slots // tmb
    grid_spec = pltpu.PrefetchScalarGridSpec(
        num_scalar_prefetch=2,
        grid=(n_blocks,),
        in_specs=[
            pl.BlockSpec((tmb, d), lambda i, be, nu: (i, 0)),
            pl.BlockSpec((1, 1, d, f2), lambda i, be, nu: (layer, be[i], 0, 0)),
            pl.BlockSpec((1, 1, 1, f2), lambda i, be, nu: (layer, be[i], 0, 0)),
            pl.BlockSpec((1, 1, f, d), lambda i, be, nu: (layer, be[i], 0, 0)),
            pl.BlockSpec((1, 1, 1, d), lambda i, be, nu: (layer, be[i], 0, 0)),
        ],
        out_specs=pl.BlockSpec((tmb, d), lambda i, be, nu: (i, 0)),
        scratch_shapes=[pltpu.VMEM((d, f2), BF16), pltpu.VMEM((f, d), BF16)],
    )
    return pl.pallas_call(
        _experts_kernel,
        grid_spec=grid_spec,
        out_shape=jax.ShapeDtypeStruct((n_slots, d), F32),
        compiler_params=_params("arbitrary"),
        name="moe_experts",
    )(block_expert, n_used, x_sorted, w1, b1.reshape(b1.shape[0], n_exp, 1, f2), w2, b2.reshape(b2.shape[0], n_exp, 1, d))


SUBLANES = 8


def _for_each_row(tm, fn):
    def body(g, carry):
        base = pl.multiple_of(g * SUBLANES, SUBLANES)
        for j in range(SUBLANES):
            fn(base + j)
        return carry

    lax.fori_loop(0, tm // SUBLANES, body, 0)


def _dispatch_kernel(dest_ref, h_ref, xs_in_ref, xs_ref, sem):
    del xs_in_ref
    tm = h_ref.shape[0]

    def row_copy(t, slot):
        return pltpu.make_async_copy(h_ref.at[pl.ds(t, 1)], xs_ref.at[pl.ds(slot, 1)], sem)

    def start(t):
        for k in range(TOP_K):
            row_copy(t, dest_ref[t * TOP_K + k]).start(priority=k % 2)

    def wait(t):
        for k in range(TOP_K):
            row_copy(t, dest_ref[t * TOP_K + k]).wait()

    _for_each_row(tm, start)
    _for_each_row(tm, wait)


def _dispatch(rows, h, dest_flat, x_sorted):
    d = rows.d
    tm = rows.tm
    return pl.pallas_call(
        _dispatch_kernel,
        grid=(rows.grid,),
        in_specs=[pl.BlockSpec((tm * TOP_K,), lambda i: (i,), memory_space=pltpu.SMEM),
                  rows.x_spec(), pl.BlockSpec(memory_space=pl.ANY)],
        out_specs=pl.BlockSpec(memory_space=pl.ANY),
        out_shape=jax.ShapeDtypeStruct(x_sorted.shape, x_sorted.dtype),
        scratch_shapes=[pltpu.SemaphoreType.DMA(())],
        input_output_aliases={2: 0},
        compiler_params=_params("arbitrary"),
        name="moe_dispatch",
    )(dest_flat, h, x_sorted)


def _combine_kernel(final, n_steps, dcur_ref, dnext_ref, x_ref, gt_ref, gates_ref, ys_ref, *rest):
    if final:
        fg_ref, o_ref, buf_ref, sems = rest
    else:
        o_ref, buf_ref, sems = rest
    i = pl.program_id(0)
    tm = x_ref.shape[0]

    def row_copy(dref, t, k, slot):
        return pltpu.make_async_copy(ys_ref.at[pl.ds(dref[t * TOP_K + k], 1)], buf_ref.at[slot, k, pl.ds(t, 1)],
                                     sems.at[slot])

    def fetch(dref, slot):
        def start(t):
            for k in range(TOP_K):
                row_copy(dref, t, k, slot).start(priority=k % 2)
        _for_each_row(tm, start)

    slot = i % 2

    @pl.when(i == 0)
    def _():
        fetch(dcur_ref, 0)

    @pl.when(i + 1 < n_steps)
    def _():
        fetch(dnext_ref, 1 - slot)

    def wait(t):
        for k in range(TOP_K):
            row_copy(dcur_ref, t, k, slot).wait()

    _for_each_row(tm, wait)
    gates = gates_ref[...]
    y = gates[:, 0:1] * buf_ref[slot, 0]
    for k in range(1, TOP_K):
        y = y + gates[:, k:k + 1] * buf_ref[slot, k]
    xn = x_ref[...] + gt_ref[...] * y
    if final:
        xn = xn * lax.rsqrt(jnp.mean(xn * xn, axis=-1, keepdims=True) + NORM_EPS) * fg_ref[...]
    o_ref[...] = xn


def _combine(rows, x, gate2, y_slots, dest_flat, gates, final_g):
    d = rows.d
    tm = rows.tm
    n_steps = rows.grid
    final = final_g is not None
    in_specs = [pl.BlockSpec((tm * TOP_K,), lambda i: (i,), memory_space=pltpu.SMEM),
                pl.BlockSpec((tm * TOP_K,), lambda i: (jnp.minimum(i + 1, n_steps - 1),), memory_space=pltpu.SMEM),
                rows.x_spec(), rows.mod_spec(), pl.BlockSpec((tm, LANES), lambda i: (i, 0)),
                pl.BlockSpec(memory_space=pl.ANY)]
    args = [dest_flat, dest_flat, x, rows.mod(gate2), gates, y_slots]
    if final:
        in_specs.append(_const_spec((1, d)))
        args.append(final_g.reshape(1, d))
    return pl.pallas_call(
        functools.partial(_combine_kernel, final, n_steps),
        grid=(n_steps,),
        in_specs=in_specs,
        out_specs=rows.x_spec(),
        out_shape=jax.ShapeDtypeStruct((rows.n, d), F32),
        scratch_shapes=[pltpu.VMEM((2, TOP_K, tm, d), F32), pltpu.SemaphoreType.DMA((2,))],
        compiler_params=_params("arbitrary"),
        name="moe_combine",
    )(*args)


def _moe_layer(layer, rows_p, rows_s, xp, xs, g, mod_p, mod_s, w_r, b_r, w1, b1, w2, b2, final_g, tmb):
    d = rows_p.d
    n_exp = w_r.shape[1]
    zero = jnp.zeros((1, LANES), F32)
    hp, idx_p, gates_p, rank_p, cnt = _router(rows_p, xp, g, mod_p[3], mod_p[4], w_r, b_r, zero)
    hs, idx_s, gates_s, rank_s, cnt = _router(rows_s, xs, g, mod_s[3], mod_s[4], w_r, b_r, cnt)
    n = rows_p.n + rows_s.n
    counts = cnt[0, :n_exp].astype(jnp.int32)
    padded = (counts + tmb - 1) // tmb * tmb
    pad_ends = jnp.cumsum(padded)
    pad_starts = pad_ends - padded
    lane_e = jnp.arange(n_exp, dtype=jnp.int32)

    def slots(idx, rank):
        starts = jnp.sum(jnp.where(idx[:, :TOP_K, None] == lane_e, pad_starts, 0), axis=-1)
        return (starts + rank[:, :TOP_K]).reshape(-1)

    dest_p = slots(idx_p, rank_p)
    dest_s = slots(idx_s, rank_s)
    n_blocks = -(-n * TOP_K // tmb) + n_exp
    block_start = jnp.arange(n_blocks, dtype=jnp.int32) * tmb
    block_expert = jnp.minimum(jnp.sum((pad_ends[None, :] <= block_start[:, None]).astype(jnp.int32), axis=1),
                               n_exp - 1)
    n_used = (pad_ends[-1:] // tmb).astype(jnp.int32)
    x_sorted = jnp.zeros((n_blocks * tmb, d), F32)
    x_sorted = _dispatch(rows_p, hp, dest_p, x_sorted)
    x_sorted = _dispatch(rows_s, hs, dest_s, x_sorted)
    y_slots = _experts(x_sorted, block_expert, n_used, w1, b1, w2, b2, layer, tmb)
    xp = _combine(rows_p, xp, mod_p[5], y_slots, dest_p, gates_p, final_g)
    xs = _combine(rows_s, xs, mod_s[5], y_slots, dest_s, gates_s, final_g)
    return xp, xs


def _proj_res_kernel(a_ref, w_ref, x_ref, gt_ref, o_ref):
    o_ref[...] = x_ref[...] + gt_ref[...] * jnp.dot(a_ref[...], w_ref[...], preferred_element_type=F32)


def _proj_residual(rows, a, w, x, gate):
    d = rows.d
    kdim = w.shape[0]
    return pl.pallas_call(
        _proj_res_kernel,
        grid=(rows.grid,),
        in_specs=[rows.x_spec(kdim), _const_spec((kdim, d)), rows.x_spec(), rows.mod_spec()],
        out_specs=rows.x_spec(),
        out_shape=jax.ShapeDtypeStruct((rows.n, d), F32),
        compiler_params=_params("arbitrary"),
        name="proj_residual",
    )(a, w.astype(BF16), x, rows.mod(gate))


def _nt_dot(a, b, precision=None):
    return lax.dot_general(a, b, (((1,), (1,)), ((), ())), precision=precision, preferred_element_type=F32)


def _fox_proj_kernel(seq_tiles, group, x_ref, g_ref, sh_ref, sc_ref, wqkv_ref, wf_ref, bf_ref,
                     q_ref, k_ref, v_ref, lf_ref, cum_ref, carry_ref):
    i = pl.program_id(0)
    tm, d = x_ref.shape
    h = _norm_mod(x_ref[...], g_ref[...], sh_ref[...], sc_ref[...])
    qkv = _bdot(h, wqkv_ref[...])
    q_ref[...] = qkv[:, :d]
    k_ref[...] = qkv[:, d:2 * d]
    v_ref[...] = qkv[:, 2 * d:]
    z = jnp.dot(h, wf_ref[...], precision=HIGHEST, preferred_element_type=F32) + bf_ref[...]
    lf = jnp.minimum(z, 0.0) - jnp.log1p(jnp.exp(-jnp.abs(z)))
    lf_ref[...] = lf
    r = lax.broadcasted_iota(jnp.int32, (tm, tm), 0)
    c = lax.broadcasted_iota(jnp.int32, (tm, tm), 1)
    if seq_tiles:
        @pl.when(i % seq_tiles == 0)
        def _():
            carry_ref[...] = jnp.zeros_like(carry_ref)

        tri = jnp.where(c <= r, 1.0, 0.0)
        cum_ref[...] = jnp.dot(tri, lf, precision=HIGHEST, preferred_element_type=F32) + carry_ref[...]
        carry_ref[...] = carry_ref[...] + jnp.sum(lf, axis=0, keepdims=True)
    else:
        tri = jnp.where(jnp.logical_and(c <= r, r // group == c // group), 1.0, 0.0)
        cum_ref[...] = jnp.dot(tri, lf, precision=HIGHEST, preferred_element_type=F32)


def _fox_proj(rows, x, g, shift, scale, w_qkv, w_f, b_f):
    d = rows.d
    nh = w_f.shape[1]
    wf = jnp.pad(w_f, ((0, 0), (0, LANES - nh)))
    bf = jnp.pad(b_f, (0, LANES - nh)).reshape(1, LANES)
    seq_tiles = rows.seq_len // rows.tm if rows.per_seq else 0
    lane_spec = pl.BlockSpec((rows.tm, LANES), lambda i: (i, 0))
    return pl.pallas_call(
        functools.partial(_fox_proj_kernel, seq_tiles, rows.seq_len),
        grid=(rows.grid,),
        in_specs=[rows.x_spec(), _const_spec((1, d)), rows.mod_spec(), rows.mod_spec(),
                  _const_spec((d, 3 * d)), _const_spec((d, LANES)), _const_spec((1, LANES))],
        out_specs=[rows.x_spec(), rows.x_spec(), rows.x_spec(), lane_spec, lane_spec],
        out_shape=[jax.ShapeDtypeStruct((rows.n, d), F32)] * 3 + [jax.ShapeDtypeStruct((rows.n, LANES), F32)] * 2,
        scratch_shapes=[pltpu.VMEM((1, LANES), F32)],
        compiler_params=_params("arbitrary"),
        name="fox_proj",
    )(x, g.reshape(1, d), rows.mod(shift), rows.mod(scale), w_qkv.astype(BF16), wf, bf)


def _flash_kernel(qi_ref, ki_ref, q_ref, k_ref, v_ref, o_ref, m_ref, l_ref, acc_ref):
    p_id = pl.program_id(1)
    qi = qi_ref[p_id]
    ki = ki_ref[p_id]
    tq = q_ref.shape[1]
    tk = k_ref.shape[1]
    nh = q_ref.shape[2] // LANES

    @pl.when(ki == 0)
    def _():
        m_ref[...] = jnp.full(m_ref.shape, -jnp.inf, F32)
        l_ref[...] = jnp.zeros(l_ref.shape, F32)
        acc_ref[...] = jnp.zeros(acc_ref.shape, F32)

    def scores(h):
        sl = slice(h * LANES, (h + 1) * LANES)
        return _nt_dot(q_ref[0, :, sl], k_ref[0, :, sl])

    def body(masked):
        if masked:
            keep = (lax.broadcasted_iota(jnp.int32, (tq, tk), 1) <= lax.broadcasted_iota(jnp.int32, (tq, tk), 0))
        s_next = scores(0)
        for h in range(nh):
            sl = slice(h * LANES, (h + 1) * LANES)
            s = s_next
            if h + 1 < nh:
                s_next = scores(h + 1)
            if masked:
                s = jnp.where(keep, s, -jnp.inf)
            m_prev = m_ref[h]
            m_new = jnp.maximum(m_prev, jnp.max(s, axis=-1, keepdims=True))
            corr = jnp.exp(m_prev - m_new)
            p = jnp.exp(s - jnp.tile(m_new, (1, tk // LANES)))
            l_ref[h] = l_ref[h] * corr + jnp.sum(p, axis=-1, keepdims=True)
            acc_ref[:, sl] = acc_ref[:, sl] * corr + jnp.dot(p.astype(BF16), v_ref[0, :, sl],
                                                            preferred_element_type=F32)
            m_ref[h] = m_new

    @pl.when(ki < qi)
    def _():
        body(False)

    @pl.when(ki == qi)
    def _():
        body(True)
        for h in range(nh):
            sl = slice(h * LANES, (h + 1) * LANES)
            o_ref[0, :, sl] = (acc_ref[:, sl] / l_ref[h]).astype(o_ref.dtype)


def _bf16_parts(c):
    def top16(x):
        bits = lax.bitcast_convert_type(x, jnp.uint32) & jnp.uint32(0xFFFF0000)
        return lax.bitcast_convert_type(bits, F32)

    c1 = top16(c)
    r1 = c - c1
    c2 = top16(r1)
    return c1, c2, top16(r1 - c2)


def _fox_head_layout(q, k, v, cum, nh):
    b, t, d = q.shape
    hd = d // nh
    parts = [p[..., None] for p in _bf16_parts(cum[:, :, :nh])]
    one = jnp.ones((b, t, nh, 1), F32)
    fill = jnp.zeros((b, t, nh, LANES - hd - 6), F32)
    qh = (q * hd ** -0.5).reshape(b, t, nh, hd)
    q_l = jnp.concatenate([qh] + parts + [one] * 3 + [fill], axis=-1)
    k_l = jnp.concatenate([k.reshape(b, t, nh, hd)] + [one] * 3 + [-p for p in parts] + [fill], axis=-1)
    v_l = jnp.pad(v.reshape(b, t, nh, hd), ((0, 0), (0, 0), (0, 0), (0, LANES - hd)))
    return tuple(z.astype(BF16).reshape(b, t, nh * LANES) for z in (q_l, k_l, v_l))


def _fox_flash(q_l, k_l, v_l, tile):
    b, t, dl = q_l.shape
    nh = dl // LANES
    tq = _tile(t, tile)
    nq = t // tq
    pairs = [(a, c) for a in range(nq) for c in range(a + 1)]
    qi = jnp.array([p[0] for p in pairs], jnp.int32)
    ki = jnp.array([p[1] for p in pairs], jnp.int32)
    q_spec = pl.BlockSpec((1, tq, dl), lambda bi, p, qi, ki: (bi, qi[p], 0))
    k_spec = pl.BlockSpec((1, tq, dl), lambda bi, p, qi, ki: (bi, ki[p], 0))
    grid_spec = pltpu.PrefetchScalarGridSpec(
        num_scalar_prefetch=2,
        grid=(b, len(pairs)),
        in_specs=[q_spec, k_spec, k_spec],
        out_specs=q_spec,
        scratch_shapes=[pltpu.VMEM((nh, tq, LANES), F32), pltpu.VMEM((nh, tq, LANES), F32),
                        pltpu.VMEM((tq, dl), F32)],
    )
    return pl.pallas_call(
        _flash_kernel,
        grid_spec=grid_spec,
        out_shape=jax.ShapeDtypeStruct((b, t, dl), BF16),
        compiler_params=_params("arbitrary", "arbitrary"),
        name="fox_flash",
    )(qi, ki, q_l, k_l, v_l)


def _paged_kernel(n_steps, nh, per_step, pt_ref, q_ref, kn_ref, vn_ref, g_ref, *rest):
    kc_refs, vc_refs, lfc_refs = (rest[j * per_step:(j + 1) * per_step] for j in range(3))
    o_ref, qbd_ref, acc_ref, m_ref, l_ref, carry_ref, lfp_ref = rest[3 * per_step:]
    s_id = pl.program_id(1)
    dseq, d = q_ref.shape
    page = kc_refs[0].shape[1]
    rws = nh * dseq
    hd = d // nh
    row_h = lax.broadcasted_iota(jnp.int32, (rws, LANES), 0) // dseq
    lane = lax.broadcasted_iota(jnp.int32, (rws, LANES), 1)
    head_sel = jnp.where(lane == row_h, 1.0, 0.0)
    g_new = g_ref[...]
    gq = jnp.sum(head_sel * jnp.concatenate([g_new] * nh, axis=0), axis=-1, keepdims=True)

    def online(s, v_bf):
        m_prev = m_ref[...]
        m_new = jnp.maximum(m_prev, jnp.max(s, axis=-1, keepdims=True))
        corr = jnp.exp(m_prev - m_new)
        p = jnp.exp(s - jnp.tile(m_new, (1, s.shape[1] // LANES)))
        l_ref[...] = l_ref[...] * corr + jnp.sum(p, axis=-1, keepdims=True)
        acc_ref[...] = acc_ref[...] * jnp.tile(corr, (1, d // LANES)) + jnp.dot(p.astype(BF16), v_bf,
                                                                                preferred_element_type=F32)
        m_ref[...] = m_new

    @pl.when(s_id == 0)
    def _():
        on_head = (lax.broadcasted_iota(jnp.int32, (rws, d), 0) // dseq
                   == lax.broadcasted_iota(jnp.int32, (rws, d), 1) // hd)
        q_rows = jnp.concatenate([q_ref[...] * (hd ** -0.5)] * nh, axis=0)
        qbd_ref[...] = jnp.where(on_head, q_rows, 0.0).astype(BF16)
        m_ref[...] = jnp.full(m_ref.shape, -jnp.inf, F32)
        l_ref[...] = jnp.zeros(l_ref.shape, F32)
        acc_ref[...] = jnp.zeros(acc_ref.shape, F32)
        carry_ref[...] = jnp.zeros(carry_ref.shape, F32)
        lfp_ref[...] = jnp.zeros(lfp_ref.shape, F32)
        pad = jnp.zeros((page - dseq, d), F32)
        k_pad = jnp.concatenate([kn_ref[...], pad], axis=0).astype(BF16)
        v_pad = jnp.concatenate([vn_ref[...], pad], axis=0).astype(BF16)
        g_pad = jnp.concatenate([g_new, jnp.zeros((page - dseq, LANES), F32)], axis=0)
        gk = _nt_dot(head_sel, g_pad, HIGHEST)
        t_row = lax.broadcasted_iota(jnp.int32, (rws, page), 0) % dseq
        u_col = lax.broadcasted_iota(jnp.int32, (rws, page), 1)
        s = _nt_dot(qbd_ref[...], k_pad) + (gq - gk)
        online(jnp.where(u_col <= t_row, s, -jnp.inf), v_pad)

    @pl.when(s_id > 0)
    def _():
        pj = lax.broadcasted_iota(jnp.int32, (page, page), 0)
        jj = lax.broadcasted_iota(jnp.int32, (page, page), 1)
        later = jnp.where(jj > pj, 1.0, 0.0)
        carry = carry_ref[...]
        biases = []
        for j in range(per_step):
            lfp_ref[j, :, 0:nh] = lfc_refs[j][0]
            lfp = lfp_ref[j]
            tail_t = jnp.dot(later, lfp, precision=HIGHEST, preferred_element_type=F32) + carry
            biases.append(_nt_dot(head_sel, tail_t, HIGHEST) + gq)
            carry = carry + jnp.sum(lfp, axis=0, keepdims=True)
        carry_ref[...] = carry
        q_bd = qbd_ref[...]
        s = jnp.concatenate([_nt_dot(q_bd, kc_refs[j][0].astype(BF16)) + biases[j] for j in range(per_step)], axis=1)
        online(s, jnp.concatenate([vc_refs[j][0].astype(BF16) for j in range(per_step)], axis=0))

    @pl.when(s_id == n_steps - 1)
    def _():
        on_head = (lax.broadcasted_iota(jnp.int32, (rws, d), 0) // dseq
                   == lax.broadcasted_iota(jnp.int32, (rws, d), 1) // hd)
        o_full = jnp.where(on_head, acc_ref[...] / jnp.tile(l_ref[...], (1, d // LANES)), 0.0)
        o_ref[...] = jnp.sum(o_full.reshape(nh, dseq, d), axis=0).astype(o_ref.dtype)


def _fox_paged(q, k_new, v_new, cum_new, cache_k, cache_v, cache_lf, page_table, dseq, layer):
    ns, d = q.shape
    bsz, n_pages = page_table.shape
    n_layers, pool, page, nh = cache_lf.shape
    rws = nh * dseq
    per_step = PAGES_PER_STEP if n_pages % PAGES_PER_STEP == 0 else 1
    n_steps = n_pages // per_step + 1

    def page_spec(j, last):
        def index(bi, s, pt):
            logical = n_pages - 1 - ((jnp.maximum(s, 1) - 1) * per_step + j)
            return (layer * pool + pt[bi * n_pages + logical], 0, 0)
        return pl.BlockSpec((1, page, last), index)

    seq_spec = pl.BlockSpec((dseq, d), lambda bi, s, pt: (bi, 0))
    grid_spec = pltpu.PrefetchScalarGridSpec(
        num_scalar_prefetch=1,
        grid=(bsz, n_steps),
        in_specs=([seq_spec, seq_spec, seq_spec, pl.BlockSpec((dseq, LANES), lambda bi, s, pt: (bi, 0))]
                  + [page_spec(j, d) for j in range(per_step)] * 2 + [page_spec(j, nh) for j in range(per_step)]),
        out_specs=seq_spec,
        scratch_shapes=[pltpu.VMEM((rws, d), BF16), pltpu.VMEM((rws, d), F32), pltpu.VMEM((rws, LANES), F32),
                        pltpu.VMEM((rws, LANES), F32), pltpu.VMEM((1, LANES), F32),
                        pltpu.VMEM((per_step, page, LANES), F32)],
    )
    kc = cache_k.astype(BF16).reshape(n_layers * pool, page, d)
    vc = cache_v.astype(BF16).reshape(n_layers * pool, page, d)
    lfc = cache_lf.reshape(n_layers * pool, page, nh)
    return pl.pallas_call(
        functools.partial(_paged_kernel, n_steps, nh, per_step),
        grid_spec=grid_spec,
        out_shape=jax.ShapeDtypeStruct((ns, d), BF16),
        compiler_params=_params("arbitrary", "arbitrary"),
        name="fox_paged",
    )(page_table.reshape(-1), q, k_new, v_new, cum_new, *([kc] * per_step), *([vc] * per_step), *([lfc] * per_step))


def _fox_layer(layer, rows_p, rows_s, xp, xs, g, mod_p, mod_s, cache_k, cache_v, cache_lf, page_table,
               w_qkv, w_f, b_f, w_o, flash_tile):
    d = rows_p.d
    nh = w_f.shape[1]
    bp, t = rows_p.n_seq, rows_p.seq_len
    qp, kp, vp, lfp, cump = _fox_proj(rows_p, xp, g, mod_p[0], mod_p[1], w_qkv, w_f, b_f)
    qs, ks, vs, lfs, cums = _fox_proj(rows_s, xs, g, mod_s[0], mod_s[1], w_qkv, w_f, b_f)
    q_l, k_l, v_l = _fox_head_layout(qp.reshape(bp, t, d), kp.reshape(bp, t, d), vp.reshape(bp, t, d),
                                     cump.reshape(bp, t, LANES), nh)
    op = _fox_flash(q_l, k_l, v_l, flash_tile)
    os_ = _fox_paged(qs, ks, vs, cums, cache_k, cache_v, cache_lf, page_table, rows_s.seq_len, layer)
    hd = d // nh
    w_o_l = jnp.pad(w_o.reshape(nh, hd, d), ((0, 0), (0, LANES - hd), (0, 0))).reshape(nh * LANES, d)
    xp = _proj_residual(rows_p, op.reshape(-1, nh * LANES), w_o_l, xp, mod_p[2])
    xs = _proj_residual(rows_s, os_, w_o, xs, mod_s[2])
    return xp, xs, (kp, vp, lfp[:, :nh]), (ks, vs, lfs[:, :nh])


def _log_sigmoid(z):
    return jnp.minimum(z, 0.0) - jnp.log1p(jnp.exp(-jnp.abs(z)))


def _head_sum(x, hd):
    i = lax.broadcasted_iota(jnp.int32, (LANES, LANES), 0) // hd
    j = lax.broadcasted_iota(jnp.int32, (LANES, LANES), 1) // hd
    same = jnp.where(i == j, 1.0, 0.0)
    parts = [jnp.dot(x[:, c * LANES:(c + 1) * LANES], same, precision=HIGHEST, preferred_element_type=F32)
             for c in range(x.shape[1] // LANES)]
    return jnp.concatenate(parts, axis=1)


def _rwkv_proj_kernel(seq_tiles, group, x_ref, xprev_ref, sh_ref, g_ref, shift_ref, scale_ref, mu_ref, wrkv_ref,
                      w0_ref, w1_ref, w2_ref, a0_ref, a1_ref, a2_ref, g1_ref, g2_ref, kk_ref, ka_ref, rk_ref,
                      r_out, lw_out, k_out, v_out, a_out, b_out, bonus_out, gate_out, h_out):
    i = pl.program_id(0)
    tm, d = x_ref.shape
    gvec, shift, scale = g_ref[...], shift_ref[...], scale_ref[...]
    h = _norm_mod(x_ref[...], gvec, shift, scale)
    rolled = pltpu.roll(h, 1, 0)
    row = lax.broadcasted_iota(jnp.int32, (tm, d), 0)
    if seq_tiles:
        h_before = _norm_mod(xprev_ref[...], gvec, shift, scale)[7:8]
        first = jnp.where(i % seq_tiles == 0, sh_ref[...], h_before)
        x_prev = jnp.where(row == 0, first, rolled)
        h_out[...] = h[tm - 8:]
    else:
        x_prev = jnp.where(row % group == 0, sh_ref[...], rolled)
        h_out[...] = h
    xx = x_prev - h
    xr, xw, xk, xv, xa, xg = (h + xx * mu_ref[n:n + 1] for n in range(6))
    r = _bdot(xr, wrkv_ref[0])
    k = _bdot(xk, wrkv_ref[1])
    v = _bdot(xv, wrkv_ref[2])
    w_log = _log_sigmoid(w0_ref[...] + _bdot(jnp.tanh(_bdot(xw, w1_ref[...])), w2_ref[...])) - 0.5
    iclr = jax.nn.sigmoid(a0_ref[...] + _bdot(_bdot(xa, a1_ref[...]), a2_ref[...]))
    gate_out[...] = _bdot(jax.nn.sigmoid(_bdot(xg, g1_ref[...])), g2_ref[...])
    kk = k * kk_ref[...]
    kk = kk / jnp.maximum(jnp.sqrt(_head_sum(kk * kk, RW_HEAD)), 1e-12)
    k2 = k * (1.0 + (iclr - 1.0) * ka_ref[...])
    r_out[...] = r
    lw_out[...] = -jnp.exp(w_log)
    k_out[...] = k2
    v_out[...] = v
    a_out[...] = -kk
    b_out[...] = kk * iclr
    bonus_out[...] = _head_sum(r * k2 * rk_ref[...], RW_HEAD) * v


def _pad_to(x, axis, size):
    pad = [(0, 0)] * x.ndim
    pad[axis] = (0, size - x.shape[axis])
    return jnp.pad(x, pad)


def _rwkv_proj(rows, x, g, shift, scale, shift_state, mu, w_rkv, w0, w1, w2, a0, a1, a2, g1, g2, k_k, k_a, r_k):
    d = rows.d
    lw_, la_, lg_ = (-(-w.shape[1] // LANES) * LANES for w in (w1, a1, g1))
    seq_tiles = rows.seq_len // rows.tm if rows.per_seq else 0
    if rows.per_seq:
        per8 = rows.tm // 8
        xprev_spec = pl.BlockSpec((8, d), lambda i: (jnp.maximum(i * per8 - 1, 0), 0))
        h_shape = jax.ShapeDtypeStruct((rows.grid * 8, d), F32)
        h_spec = pl.BlockSpec((8, d), lambda i: (i, 0))
    else:
        xprev_spec = pl.BlockSpec((8, d), lambda i: (0, 0))
        h_shape = jax.ShapeDtypeStruct((rows.n, d), F32)
        h_spec = rows.x_spec()
    vec = _const_spec((1, d))
    out = pl.pallas_call(
        functools.partial(_rwkv_proj_kernel, seq_tiles, rows.seq_len),
        grid=(rows.grid,),
        in_specs=[rows.x_spec(), xprev_spec, rows.mod_spec(), vec, rows.mod_spec(), rows.mod_spec(),
                  _const_spec((6, d)), _const_spec((3, d, d)),
                  vec, _const_spec((d, lw_)), _const_spec((lw_, d)),
                  vec, _const_spec((d, la_)), _const_spec((la_, d)),
                  _const_spec((d, lg_)), _const_spec((lg_, d)), vec, vec, vec],
        out_specs=[rows.x_spec()] * 8 + [h_spec],
        out_shape=[jax.ShapeDtypeStruct((rows.n, d), F32)] * 8 + [h_shape],
        compiler_params=_params("arbitrary"),
        name="rwkv_proj",
    )(x, x, rows.mod(shift_state), g.reshape(1, d), rows.mod(shift), rows.mod(scale), mu, w_rkv.astype(BF16),
      w0.reshape(1, d), _pad_to(w1, 1, lw_).astype(BF16), _pad_to(w2, 0, lw_).astype(BF16),
      a0.reshape(1, d), _pad_to(a1, 1, la_).astype(BF16), _pad_to(a2, 0, la_).astype(BF16),
      _pad_to(g1, 1, lg_).astype(BF16), _pad_to(g2, 0, lg_).astype(BF16),
      k_k.reshape(1, d), k_a.reshape(1, d), r_k.reshape(1, d))
    return out


def _tn_dot(a, b):
    return lax.dot_general(a.astype(BF16), b.astype(BF16), (((0,), (0,)), ((), ())), preferred_element_type=F32)


def _wkv_kernel(chunk, r_ref, lw_ref, k_ref, v_ref, a_ref, b_ref, s0_ref, y_ref, sT_ref, st_ref):
    tt_id = pl.program_id(2)
    nseq, tt, width = r_ref.shape
    hd = RW_HEAD
    nhead = width // hd
    ln = chunk
    nchunk = tt // ln

    @pl.when(tt_id == 0)
    def _():
        st_ref[...] = s0_ref[...]

    rr = lax.broadcasted_iota(jnp.int32, (ln, ln), 0)
    cc = lax.broadcasted_iota(jnp.int32, (ln, ln), 1)
    tri = jnp.where(cc <= rr, 1.0, 0.0)
    eye = jnp.where(cc == rr, 1.0, 0.0)
    r2 = lax.broadcasted_iota(jnp.int32, (2 * ln, 2 * ln), 0)
    c2 = lax.broadcasted_iota(jnp.int32, (2 * ln, 2 * ln), 1) % ln
    keep = c2 < jnp.where(r2 < ln, r2, r2 - ln + 1)
    n_double = ln.bit_length() - 2
    items = [(s, c, j) for s in range(nseq) for c in range(nchunk) for j in range(nhead)]

    scaled = {}
    for s in range(nseq):
        for c in range(nchunk):
            rows = slice(c * ln, (c + 1) * ln)
            lw = lw_ref[s, rows, :]
            cw = jnp.dot(tri, lw, precision=HIGHEST, preferred_element_type=F32)
            e_neg = jnp.exp(-cw)
            e_all = jnp.exp(cw[ln - 1:ln, :])
            b_t = b_ref[s, rows, :] * e_neg
            k_t = k_ref[s, rows, :] * e_neg
            scaled[s, c] = (a_ref[s, rows, :] * jnp.exp(cw - lw), r_ref[s, rows, :] * jnp.exp(cw), b_t, k_t,
                            b_t * e_all, k_t * e_all, e_all, v_ref[s, rows, :])

    gram, a_t, r_t, vj, ends, e_all = {}, {}, {}, {}, {}, {}
    for it in items:
        s, c, j = it
        sl = slice(j * hd, (j + 1) * hd)
        at, rt, bt, kt, b_end, k_end, ea, vv = scaled[s, c]
        a_t[it], r_t[it], vj[it], e_all[it] = at[:, sl], rt[:, sl], vv[:, sl], ea[:, sl]
        ends[it] = jnp.concatenate([b_end[:, sl], k_end[:, sl]], axis=0)
        lhs = jnp.concatenate([at[:, sl], rt[:, sl]], axis=0)
        rhs = jnp.concatenate([bt[:, sl], kt[:, sl]], axis=0)
        gram[it] = jnp.where(keep, _nt_dot(lhs.astype(BF16), rhs.astype(BF16)), 0.0)
    inv = {it: eye + gram[it][:ln, :ln] for it in items}
    power = {it: gram[it][:ln, :ln] for it in items}
    for _ in range(n_double):
        power = {it: _bdot(power[it], power[it]) for it in items}
        inv = {it: inv[it] + _bdot(inv[it], power[it]) for it in items}
    x0 = {it: _bdot(gram[it][:ln, ln:], vj[it]) for it in items}
    gu = {it: _bdot(inv[it], jnp.concatenate([a_t[it], x0[it]], axis=1)) for it in items}
    uv = {it: jnp.concatenate([gu[it][:, hd:], vj[it]], axis=0) for it in items}
    q_r = {it: r_t[it] + _bdot(gram[it][ln:, :ln], gu[it][:, :hd]) for it in items}
    y0 = {it: _bdot(gram[it][ln:, :], uv[it]) for it in items}
    m_st = {it: _tn_dot(gu[it][:, :hd], ends[it][:ln]) for it in items}
    c_st = {it: _tn_dot(uv[it], ends[it]) for it in items}

    for s in range(nseq):
        states = [st_ref[s, j] for j in range(nhead)]
        for c in range(nchunk):
            ys = []
            for j in range(nhead):
                it = (s, c, j)
                ys.append(_nt_dot(q_r[it].astype(BF16), states[j].astype(BF16)) + y0[it])
                states[j] = states[j] * e_all[it] + _bdot(states[j], m_st[it]) + c_st[it]
            y_ref[s, c * ln:(c + 1) * ln, :] = jnp.concatenate(ys, axis=1)
        for j in range(nhead):
            st_ref[s, j] = states[j]

    @pl.when(tt_id == pl.num_programs(2) - 1)
    def _():
        sT_ref[...] = st_ref[...]


def _wkv(r, lw, k, v, a, b, state0, chunk, tile, nseq, width):
    bsz, t, d = r.shape
    tt = _tile(t, tile)
    assert tt % chunk == 0 and bsz % nseq == 0 and d % width == 0
    nhead = width // RW_HEAD
    seq_spec = pl.BlockSpec((nseq, tt, width), lambda bi, hp, ti: (bi, ti, hp))
    st_spec = pl.BlockSpec((nseq, nhead, RW_HEAD, RW_HEAD), lambda bi, hp, ti: (bi, hp, 0, 0))
    return pl.pallas_call(
        functools.partial(_wkv_kernel, chunk),
        grid=(bsz // nseq, d // width, t // tt),
        in_specs=[seq_spec] * 6 + [st_spec],
        out_specs=[seq_spec, st_spec],
        out_shape=[jax.ShapeDtypeStruct((bsz, t, d), F32), jax.ShapeDtypeStruct(state0.shape, F32)],
        scratch_shapes=[pltpu.VMEM((nseq, nhead, RW_HEAD, RW_HEAD), F32)],
        compiler_params=_params("arbitrary", "arbitrary", "arbitrary"),
        name="wkv",
    )(r, lw, k, v, a, b, state0)


def _rwkv_out_kernel(y_ref, bonus_ref, gate_ref, x_ref, gt_ref, lng_ref, lnb_ref, wo_ref, o_ref):
    y = y_ref[...]
    mean = _head_sum(y, RW_HEAD) * (1.0 / RW_HEAD)
    yc = y - mean
    var = _head_sum(yc * yc, RW_HEAD) * (1.0 / RW_HEAD)
    yn = yc * lax.rsqrt(var + RW_GN_EPS) * lng_ref[...] + lnb_ref[...] + bonus_ref[...]
    o_ref[...] = x_ref[...] + gt_ref[...] * _bdot(yn * gate_ref[...], wo_ref[...])


def _rwkv_out(rows, y, bonus, gate, x, gate1, ln_g, ln_b, w_o):
    d = rows.d
    return pl.pallas_call(
        _rwkv_out_kernel,
        grid=(rows.grid,),
        in_specs=[rows.x_spec()] * 4 + [rows.mod_spec(), _const_spec((1, d)), _const_spec((1, d)), _const_spec((d, d))],
        out_specs=rows.x_spec(),
        out_shape=jax.ShapeDtypeStruct((rows.n, d), F32),
        compiler_params=_params("arbitrary"),
        name="rwkv_out",
    )(y, bonus, gate, x, rows.mod(gate1), ln_g.reshape(1, d), ln_b.reshape(1, d), w_o.astype(BF16))


def _rwkv_group(rows, x, g, mod, shift_state, wkv_state, mu, w_rkv, w_o, w0, w1, w2, a0, a1, a2, g1, g2,
                k_k, k_a, r_k, ln_g, ln_b, chunk, tile, nseq):
    d = rows.d
    bsz, t = rows.n_seq, rows.seq_len
    r, lw, k, v, a, b, bonus, gate, h_tail = _rwkv_proj(rows, x, g, mod[0], mod[1], shift_state, mu, w_rkv,
                                                        w0, w1, w2, a0, a1, a2, g1, g2, k_k, k_a, r_k)
    to3 = lambda z: z.reshape(bsz, t, d)
    y, state = _wkv(to3(r), to3(lw), to3(k), to3(v), to3(a), to3(b), wkv_state, chunk, tile, nseq, WKV_LANES)
    x_new = _rwkv_out(rows, y.reshape(-1, d), bonus, gate, x, mod[2], ln_g, ln_b, w_o)
    if rows.per_seq:
        h_last = h_tail.reshape(bsz, -1, d)[:, -1]
    else:
        h_last = h_tail.reshape(bsz, t, d)[:, -1]
    return x_new, h_last, state


def kernel(x_prompt, x_sample, cache_fox_k, cache_fox_v, cache_fox_logf, state_rwkv_shift, state_rwkv_wkv, page_table, c_prompt, c_sample, norm1_g, norm2_g, ada_w, ada_b, cm_w_in, cm_b_in, cm_ln_g, cm_ln_b, cm_w_s, cm_b_s, cm_w_out, cm_b_out, rw_mu, rw_w_rkv, rw_w_o, rw_w0, rw_w1, rw_w2, rw_a0, rw_a1, rw_a2, rw_g1, rw_g2, rw_k_k, rw_k_a, rw_r_k, rw_ln_g, rw_ln_b, fox_w_qkv, fox_w_f, fox_b_f, fox_w_o, moe_wr, moe_br, moe_w1, moe_b1, moe_w2, moe_b2, final_g):
    bp, seq, d = x_prompt.shape
    bs, dseq, _ = x_sample.shape
    c_all = jnp.concatenate([c_prompt, c_sample], 0)
    n = c_all.shape[0]
    npad = -(-n // 8) * 8
    mod = _adaln(jnp.pad(c_all, ((0, npad - n), (0, 0))), ada_w, ada_b)
    depth = ada_w.shape[0]
    rows_p = _Rows(bp, seq, d, ROW_TILE)
    rows_s = _Rows(bs, dseq, d, ROW_TILE)
    width = cm_w_out.shape[1]
    nh_rw = d // RW_HEAD
    nh_fox = fox_w_f.shape[2]
    xp = x_prompt.reshape(-1, d)
    xs = x_sample.reshape(-1, d)
    cm_v_s = []
    rw_sh_p, rw_wkv_p, rw_sh_s, rw_wkv_s = [], [], [], []
    fx_p, fx_s = [], []
    for i in range(depth):
        kind, j = i % N_MIXERS, i // N_MIXERS
        mp = jnp.split(mod[i, :bp], 6, axis=-1)
        ms = jnp.split(mod[i, bp:n], 6, axis=-1)
        if kind == 0:
            cm = (cm_w_in[j], cm_b_in[j], cm_ln_g[j], cm_ln_b[j])
            ws_p, bs_p = _cm_spatial_weights(cm_w_s[j], cm_b_s[j], CHUNK, width)
            ws_s, bs_s = _cm_spatial_weights(cm_w_s[j], cm_b_s[j], dseq, width)
            (xp,) = _cm_layer(rows_p, xp, norm1_g[i], mp[0], mp[1], mp[2], *cm, ws_p, bs_p, cm_w_out[j], cm_b_out[j], False)
            xs, v_new = _cm_layer(rows_s, xs, norm1_g[i], ms[0], ms[1], ms[2], *cm, ws_s, bs_s, cm_w_out[j], cm_b_out[j], True)
            cm_v_s.append(v_new.reshape(bs, dseq, width))
        elif kind == 1:
            rw = (rw_mu[j], rw_w_rkv[j], rw_w_o[j], rw_w0[j], rw_w1[j], rw_w2[j], rw_a0[j], rw_a1[j], rw_a2[j],
                  rw_g1[j], rw_g2[j], rw_k_k[j], rw_k_a[j], rw_r_k[j], rw_ln_g[j], rw_ln_b[j])
            xp, sh_p, wkv_p = _rwkv_group(rows_p, xp, norm1_g[i], mp, jnp.zeros((bp, d), F32),
                                          jnp.zeros((bp, nh_rw, RW_HEAD, RW_HEAD), F32), *rw,
                                          min(WKV_CHUNK, seq), WKV_TILE, 1)
            xs, sh_s, wkv_s = _rwkv_group(rows_s, xs, norm1_g[i], ms, state_rwkv_shift[j], state_rwkv_wkv[j], *rw,
                                          dseq, dseq, min(WKV_DECODE_SEQS, bs))
            rw_sh_p.append(sh_p)
            rw_wkv_p.append(wkv_p)
            rw_sh_s.append(sh_s)
            rw_wkv_s.append(wkv_s)
        else:
            xp, xs, (k_p, v_p, lf_p), (k_s, v_s, lf_s) = _fox_layer(
                j, rows_p, rows_s, xp, xs, norm1_g[i], mp, ms, cache_fox_k, cache_fox_v, cache_fox_logf,
                page_table, fox_w_qkv[j], fox_w_f[j], fox_b_f[j], fox_w_o[j], FLASH_TILE)
            hd = d // nh_fox
            fx_p.append((k_p.reshape(bp, seq, nh_fox, hd), v_p.reshape(bp, seq, nh_fox, hd),
                         lf_p.reshape(bp, seq, nh_fox)))
            fx_s.append((k_s.reshape(bs, dseq, nh_fox, hd), v_s.reshape(bs, dseq, nh_fox, hd),
                         lf_s.reshape(bs, dseq, nh_fox)))
        xp, xs = _moe_layer(i, rows_p, rows_s, xp, xs, norm2_g[i], mp, ms, moe_wr[i], moe_br[i],
                            moe_w1, moe_b1, moe_w2, moe_b2, final_g if i == depth - 1 else None, MOE_ROWS)
    return (xp.reshape(bp, seq, d), xs.reshape(bs, dseq, d), jnp.stack(cm_v_s),
            jnp.stack(rw_sh_p), jnp.stack(rw_wkv_p), jnp.stack(rw_sh_s), jnp.stack(rw_wkv_s),
            jnp.stack([f[0] for f in fx_p]), jnp.stack([f[1] for f in fx_p]), jnp.stack([f[2] for f in fx_p]),
            jnp.stack([f[0] for f in fx_s]), jnp.stack([f[1] for f in fx_s]), jnp.stack([f[2] for f in fx_s]))
```
